```python
import math
import jax, jax.numpy as jnp
from jax import lax
import numpy as np

D_MODEL = 2048
BATCH = 4
SEQ = 2048
DEPTH = 2
DEC_BATCH = 2
DEC_SEQ = 4096
PAST_LEN = 128

HEAD_DIM = 128
RET_HEADS = 4
RET_WIDTH = RET_HEADS * HEAD_DIM
RET_CHUNK = 128
MLA_HEADS = 4
MLA_Q_RANK = 384
MLA_KV_RANK = 256
MLA_NOPE = 128
MLA_ROPE = 64
MLA_QK = MLA_NOPE + MLA_ROPE
MLA_V = 128
GQA_HEADS = 8
GQA_KV_HEADS = 2
MIX_WIDTH = RET_WIDTH + MLA_HEADS * MLA_V + GQA_HEADS * HEAD_DIM
IN_WIDTH = 4 * RET_WIDTH + MLA_Q_RANK + MLA_KV_RANK + MLA_ROPE + (GQA_HEADS + 2 * GQA_KV_HEADS) * HEAD_DIM
D_FF = 4 * D_MODEL
GRID_W = 64
Q_BLOCK = 128
ROPE_THETA = 10000.0
EPS = 1e-6

kernel_name = 'hybrid_retention_mla_axial_gqa_encoder'


def rmsnorm(x, w):
    xf = x.astype(jnp.float32)
    y = xf * lax.rsqrt(jnp.mean(xf * xf, axis=-1, keepdims=True) + EPS)
    return (y * w.astype(jnp.float32)).astype(x.dtype)


def rope_1d(S, dim):
    inv = 1.0 / (ROPE_THETA ** (jnp.arange(0, dim, 2, dtype=jnp.float32) / dim))
    ang = jnp.arange(S, dtype=jnp.float32)[:, None] * inv[None, :]
    return jnp.cos(ang), jnp.sin(ang)


def rope_axial(S, dim):
    rows = S // GRID_W
    t = jnp.arange(S)
    row = jnp.repeat(jnp.arange(rows), GRID_W, total_repeat_length=S).astype(jnp.float32)
    col = (t % GRID_W).astype(jnp.float32)
    half = dim // 2
    inv = 1.0 / (ROPE_THETA ** (jnp.arange(0, half, 2, dtype=jnp.float32) / half))
    ang = jnp.concatenate([row[:, None] * inv[None, :], col[:, None] * inv[None, :]], axis=-1)
    return jnp.cos(ang), jnp.sin(ang)


def apply_rope(x, cos, sin):
    d2 = x.shape[-1] // 2
    xf = x.astype(jnp.float32)
    x1, x2 = xf[..., :d2], xf[..., d2:]
    c, s = cos[None, :, None, :], sin[None, :, None, :]
    return jnp.concatenate([x1 * c - x2 * s, x1 * s + x2 * c], axis=-1).astype(x.dtype)


def retention_one_direction(q, k, v, log_gamma, inclusive):
    B, H, S, D = q.shape
    E = v.shape[-1]
    C = RET_CHUNK
    N = S // C
    f32 = jnp.float32
    qc = q.reshape(B, H, N, C, D).astype(f32)
    kc = k.reshape(B, H, N, C, D).astype(f32)
    vc = v.reshape(B, H, N, C, E).astype(f32)
    pos = jnp.arange(C, dtype=f32)
    diff = pos[:, None] - pos[None, :]
    mask = diff >= 0 if inclusive else diff > 0
    decay_intra = jnp.where(mask[None], jnp.exp(jnp.where(mask, diff, 0.0)[None] * log_gamma[:, None, None]), 0.0)
    scores = jnp.einsum('bhncd,bhnmd->bhncm', qc, kc) * decay_intra[None, :, None]
    intra = jnp.einsum('bhncm,bhnme->bhnce', scores, vc)
    zeta = jnp.exp((C - 1 - pos)[None, :] * log_gamma[:, None])
    chunk_kv = jnp.einsum('bhncd,hc,bhnce->nbhde', kc, zeta, vc)
    chunk_decay = jnp.exp(C * log_gamma)[None, :, None, None]

    def step(state, kv_n):
        return chunk_decay * state + kv_n, state

    _, prev = lax.scan(step, jnp.zeros((B, H, D, E), f32), chunk_kv)
    inner = jnp.exp((pos + 1.0)[None, :] * log_gamma[:, None])
    cross = jnp.einsum('bhncd,hc,nbhde->bhnce', qc, inner, prev)
    return (intra + cross).reshape(B, H, S, E)


def blocked_attention(q, k, v, scale):
    B, S, Hq, D = q.shape
    Hkv = k.shape[2]
    G = Hq // Hkv
    E = v.shape[-1]
    nb = S // Q_BLOCK
    qb = q.reshape(B, nb, Q_BLOCK, Hkv, G, D).transpose(1, 0, 2, 3, 4, 5)

    def one_block(q_blk):
        s = jnp.einsum('bqhgd,bkhd->bhgqk', q_blk, k, preferred_element_type=jnp.float32) * scale
        p = jax.nn.softmax(s, axis=-1).astype(v.dtype)
        return jnp.einsum('bhgqk,bkhe->bqhge', p, v)

    o = lax.map(one_block, qb)
    return o.transpose(1, 0, 2, 3, 4, 5).reshape(B, S, Hq * E)


def encoder_layer(x, ret_cs, ax128_cs, ax64_cs, ln1_w, w_in, ret_decay_fwd, ret_decay_bwd, ret_gn_w,
                  mla_q_a_norm, mla_w_uq, mla_kv_a_norm, mla_w_ukv, mla_q_norm, mla_k_norm,
                  gqa_q_norm, gqa_k_norm, w_out, ln2_w, w_up, w_down):
    B, S, _ = x.shape
    f32 = jnp.float32
    h = rmsnorm(x, ln1_w)
    proj = h @ w_in
    sizes = (RET_WIDTH, RET_WIDTH, RET_WIDTH, RET_WIDTH, MLA_Q_RANK, MLA_KV_RANK, MLA_ROPE,
             GQA_HEADS * HEAD_DIM, GQA_KV_HEADS * HEAD_DIM, GQA_KV_HEADS * HEAD_DIM)
    offsets = np.cumsum(sizes)[:-1].tolist()
    rq, rk, rv, rg, cq, ckv, krope, gq, gk, gv = jnp.split(proj, offsets, axis=-1)

    cos_r, sin_r = ret_cs
    rq = apply_rope(rq.reshape(B, S, RET_HEADS, HEAD_DIM), cos_r, sin_r).transpose(0, 2, 1, 3)
    rk = (apply_rope(rk.reshape(B, S, RET_HEADS, HEAD_DIM), cos_r, sin_r) * (HEAD_DIM ** -0.5)).transpose(0, 2, 1, 3)
    rv = rv.reshape(B, S, RET_HEADS, HEAD_DIM).transpose(0, 2, 1, 3)
    log_gf = jax.nn.log_sigmoid(ret_decay_fwd.astype(f32))
    log_gb = jax.nn.log_sigmoid(ret_decay_bwd.astype(f32))
    y_f = retention_one_direction(rq, rk, rv, log_gf, True)
    y_b = jnp.flip(retention_one_direction(jnp.flip(rq, 2), jnp.flip(rk, 2), jnp.flip(rv, 2), log_gb, False), 2)
    y = (y_f + y_b).transpose(0, 2, 1, 3)
    mu = jnp.mean(y, axis=-1, keepdims=True)
    var = jnp.mean(jnp.square(y - mu), axis=-1, keepdims=True)
    y = ((y - mu) * lax.rsqrt(var + EPS)).reshape(B, S, RET_WIDTH)
    ret_out = (y * ret_gn_w.astype(f32) * jax.nn.silu(rg.astype(f32))).astype(x.dtype)

    cos64, sin64 = ax64_cs
    q_m = (rmsnorm(cq, mla_q_a_norm) @ mla_w_uq).reshape(B, S, MLA_HEADS, MLA_QK)
    kv_m = (rmsnorm(ckv, mla_kv_a_norm) @ mla_w_ukv).reshape(B, S, MLA_HEADS, MLA_NOPE + MLA_V)
    k_nope, v_m = kv_m[..., :MLA_NOPE], kv_m[..., MLA_NOPE:]
    k_rope = jnp.broadcast_to(krope[:, :, None, :], (B, S, MLA_HEADS, MLA_ROPE))
    k_m = jnp.concatenate([k_nope, k_rope], axis=-1)
    q_m = rmsnorm(q_m, mla_q_norm)
    k_m = rmsnorm(k_m, mla_k_norm)
    q_m = jnp.concatenate([q_m[..., :MLA_NOPE], apply_rope(q_m[..., MLA_NOPE:], cos64, sin64)], axis=-1)
    k_m = jnp.concatenate([k_m[..., :MLA_NOPE], apply_rope(k_m[..., MLA_NOPE:], cos64, sin64)], axis=-1)
    mla_out = blocked_attention(q_m, k_m, v_m, MLA_QK ** -0.5)

    cos128, sin128 = ax128_cs
    q_g = apply_rope(rmsnorm(gq.reshape(B, S, GQA_HEADS, HEAD_DIM), gqa_q_norm), cos128, sin128)
    k_g = apply_rope(rmsnorm(gk.reshape(B, S, GQA_KV_HEADS, HEAD_DIM), gqa_k_norm), cos128, sin128)
    v_g = gv.reshape(B, S, GQA_KV_HEADS, HEAD_DIM)
    gqa_out = blocked_attention(q_g, k_g, v_g, HEAD_DIM ** -0.5)

    mix = jnp.concatenate([ret_out, mla_out.astype(x.dtype), gqa_out.astype(x.dtype)], axis=-1)
    x = x + mix @ w_out

    h2 = rmsnorm(x, ln2_w)
    u = jnp.square(jax.nn.relu(h2 @ w_up))
    return x + u @ w_down


def setup_inputs(seed: int = 0) -> dict:
    key = jax.random.key(seed)
    ks = jax.random.split(key, 20)
    f32 = jnp.float32

    def normal(k, shape, scale):
        return jax.random.normal(k, shape, f32) * scale

    def gain(k, shape):
        return 1.0 + 0.02 * jax.random.normal(k, shape, f32)

    gamma0 = 1.0 - 2.0 ** (-5.0 - jnp.arange(RET_HEADS, dtype=f32))
    logit0 = jnp.log(gamma0) - jnp.log1p(-gamma0)
    return {
        'x_prompt': jax.random.normal(ks[0], (BATCH, SEQ, D_MODEL), f32),
        'x_sample': jax.random.normal(ks[1], (DEC_BATCH, DEC_SEQ, D_MODEL), f32),
        'ln1_w': gain(ks[2], (DEPTH, D_MODEL)),
        'w_in': normal(ks[3], (DEPTH, D_MODEL, IN_WIDTH), D_MODEL ** -0.5),
        'ret_decay_fwd': logit0[None, :] + 0.1 * jax.random.normal(ks[4], (DEPTH, RET_HEADS), f32),
        'ret_decay_bwd': logit0[None, :] + 0.1 * jax.random.normal(ks[5], (DEPTH, RET_HEADS), f32),
        'ret_gn_w': gain(ks[6], (DEPTH, RET_WIDTH)),
        'mla_q_a_norm': gain(ks[7], (DEPTH, MLA_Q_RANK)),
        'mla_w_uq': normal(ks[8], (DEPTH, MLA_Q_RANK, MLA_HEADS * MLA_QK), MLA_Q_RANK ** -0.5),
        'mla_kv_a_norm': gain(ks[9], (DEPTH, MLA_KV_RANK)),
        'mla_w_ukv': normal(ks[10], (DEPTH, MLA_KV_RANK, MLA_HEADS * (MLA_NOPE + MLA_V)), MLA_KV_RANK ** -0.5),
        'mla_q_norm': gain(ks[11], (DEPTH, MLA_QK)),
        'mla_k_norm': gain(ks[12], (DEPTH, MLA_QK)),
        'gqa_q_norm': gain(ks[13], (DEPTH, HEAD_DIM)),
        'gqa_k_norm': gain(ks[14], (DEPTH, HEAD_DIM)),
        'w_out': normal(ks[15], (DEPTH, MIX_WIDTH, D_MODEL), MIX_WIDTH ** -0.5),
        'ln2_w': gain(ks[16], (DEPTH, D_MODEL)),
        'w_up': normal(ks[17], (DEPTH, D_MODEL, D_FF), D_MODEL ** -0.5),
        'w_down': normal(ks[18], (DEPTH, D_FF, D_MODEL), D_FF ** -0.5),
    }


def reference(x_prompt, x_sample, ln1_w, w_in, ret_decay_fwd, ret_decay_bwd, ret_gn_w,
              mla_q_a_norm, mla_w_uq, mla_kv_a_norm, mla_w_ukv, mla_q_norm, mla_k_norm,
              gqa_q_norm, gqa_k_norm, w_out, ln2_w, w_up, w_down):
    def trunk(x):
        S = x.shape[1]
        ret_cs = rope_1d(S, HEAD_DIM)
        ax128_cs = rope_axial(S, HEAD_DIM)
        ax64_cs = rope_axial(S, MLA_ROPE)
        for l in range(DEPTH):
            x = encoder_layer(x, ret_cs, ax128_cs, ax64_cs, ln1_w[l], w_in[l], ret_decay_fwd[l], ret_decay_bwd[l],
                              ret_gn_w[l], mla_q_a_norm[l], mla_w_uq[l], mla_kv_a_norm[l], mla_w_ukv[l],
                              mla_q_norm[l], mla_k_norm[l], gqa_q_norm[l], gqa_k_norm[l], w_out[l],
                              ln2_w[l], w_up[l], w_down[l])
        return x

    y_prompt = trunk(x_prompt)
    y_sample = trunk(x_sample)
    return (y_prompt, y_sample)
```

```python
import functools

import jax
import jax.numpy as jnp
from jax import lax
from jax.experimental import pallas as pl
from jax.experimental.pallas import tpu as pltpu

D_MODEL = 2048
HEAD_DIM = 128
RET_HEADS = 4
RET_WIDTH = RET_HEADS * HEAD_DIM
MLA_HEADS = 4
MLA_Q_RANK = 384
MLA_KV_RANK = 256
MLA_NOPE = 128
MLA_ROPE = 64
MLA_QK = MLA_NOPE + MLA_ROPE
MLA_V = 128
GQA_HEADS = 8
GQA_KV_HEADS = 2
GQA_GROUP = GQA_HEADS // GQA_KV_HEADS
D_FF = 4 * D_MODEL
GRID_W = 64
ROPE_THETA = 10000.0
EPS = 1e-6

LANES = 128
MLA_PAD = 2 * LANES

OFF_CQ = 0
OFF_KROPE = 384
OFF_CKV = 512
OFF_GK = 768
OFF_GQ = 1024
OFF_RQ = 2048
OFF_RK = 2560
OFF_RV = 3072
OFF_RG = 3584
OFF_GV = 4096
IN_PAD = 4352

TM_PROJ = 512
TF_MLP = 1024
TM_PREP = 512
TQ_ATTN = 512
RET_CHUNK = 256
VMEM_LIMIT = 52 * 1024 * 1024

F32 = jnp.float32
BF16 = jnp.bfloat16


def _params(*sem):
    return pltpu.CompilerParams(dimension_semantics=sem, vmem_limit_bytes=VMEM_LIMIT)


def _rms(x, g):
    ms = jnp.mean(x * x, axis=-1, keepdims=True)
    return x * lax.rsqrt(ms + EPS) * g


def _rope(x, c, s):
    return x * c + pltpu.roll(x, LANES // 2, 1) * s


def _inproj_kernel(x_ref, g_ref, w_ref, o_ref):
    h = _rms(x_ref[...], g_ref[...]).astype(BF16)
    n = o_ref.shape[1]
    for c0 in range(0, n, 1024):
        c1 = min(c0 + 1024, n)
        o_ref[:, c0:c1] = jnp.dot(h, w_ref[:, c0:c1], preferred_element_type=F32).astype(o_ref.dtype)


def _inproj(x, g, w):
    m = x.shape[0]
    return pl.pallas_call(
        _inproj_kernel,
        grid=(m // TM_PROJ,),
        in_specs=[
            pl.BlockSpec((TM_PROJ, D_MODEL), lambda i: (i, 0)),
            pl.BlockSpec((1, D_MODEL), lambda i: (0, 0)),
            pl.BlockSpec((D_MODEL, IN_PAD), lambda i: (0, 0), pipeline_mode=pl.Buffered(1)),
        ],
        out_specs=pl.BlockSpec((TM_PROJ, IN_PAD), lambda i: (i, 0)),
        out_shape=jax.ShapeDtypeStruct((m, IN_PAD), BF16),
        compiler_params=_params("parallel"),
        name="in_proj",
    )(x, g, w)


def _ret_kernel(q_ref, k_ref, v_ref, g_ref, cos_ref, sin_ref, dec_ref, gn_ref, o_ref, kv_scr, st_scr, *, chunk):
    seq = q_ref.shape[0]
    n_chunks = seq // chunk
    lgf = dec_ref[0, 0:1, :]
    lgb = dec_ref[0, 1:2, :]
    lgf_l = lgf[:, :LANES]
    lgb_l = lgb[:, :LANES]
    pos = lax.broadcasted_iota(jnp.int32, (chunk, LANES), 0).astype(F32)
    zeta_f = jnp.exp((chunk - 1.0 - pos) * lgf_l)
    zeta_b = jnp.exp(pos * lgb_l)
    inner_f = jnp.exp((pos + 1.0) * lgf_l)
    inner_b = jnp.exp((chunk - pos) * lgb_l)
    ii = lax.broadcasted_iota(jnp.int32, (chunk, chunk), 0)
    jj = lax.broadcasted_iota(jnp.int32, (chunk, chunk), 1)
    d = (ii - jj).astype(F32)
    dmask = jnp.where(d >= 0, jnp.exp(jnp.maximum(d, 0.0) * lgf), jnp.exp(jnp.maximum(-d, 0.0) * lgb))
    k_scale = HEAD_DIM ** -0.5

    def rows(c):
        return pl.ds(pl.multiple_of(c * chunk, chunk), chunk)

    def chunk_kv(c, carry):
        r = rows(c)
        k = _rope(k_ref[r, :].astype(F32), cos_ref[r, :], sin_ref[r, :]) * k_scale
        v = v_ref[r, :].astype(F32)
        vz = jnp.concatenate([v * zeta_f, v * zeta_b], axis=1).astype(BF16)
        kt = k.T.astype(BF16)
        kv_scr[c] = jnp.dot(kt, vz, preferred_element_type=F32)
        return carry

    lax.fori_loop(0, n_chunks, chunk_kv, 0)

    dec_f = jnp.exp(chunk * lgf_l)
    dec_b = jnp.exp(chunk * lgb_l)
    sf = jnp.zeros((HEAD_DIM, LANES), F32)
    for c in range(n_chunks):
        st_scr[c, :, :LANES] = sf.astype(BF16)
        sf = dec_f * sf + kv_scr[c, :, :LANES]
    sb = jnp.zeros((HEAD_DIM, LANES), F32)
    for c in range(n_chunks - 1, -1, -1):
        st_scr[c, :, LANES:] = sb.astype(BF16)
        sb = dec_b * sb + kv_scr[c, :, LANES:]

    gn = gn_ref[...]

    def chunk_out(c, carry):
        r = rows(c)
        cs = cos_ref[r, :]
        sn = sin_ref[r, :]
        q = _rope(q_ref[r, :].astype(F32), cs, sn).astype(BF16)
        k = (_rope(k_ref[r, :].astype(F32), cs, sn) * k_scale).astype(BF16)
        v = v_ref[r, :].astype(BF16)
        a = lax.dot_general(q, k, (((1,), (1,)), ((), ())), preferred_element_type=F32) * dmask
        y = jnp.dot(a.astype(BF16), v, preferred_element_type=F32)
        cr = jnp.dot(q, st_scr[c], preferred_element_type=F32)
        y = y + cr[:, :LANES] * inner_f + cr[:, LANES:] * inner_b
        mu = jnp.mean(y, axis=-1, keepdims=True)
        yc = y - mu
        var = jnp.mean(yc * yc, axis=-1, keepdims=True)
        yn = yc * lax.rsqrt(var + EPS)
        g = g_ref[r, :].astype(F32)
        silu = g * (1.0 / (1.0 + jnp.exp(-g)))
        o_ref[r, :] = (yn * gn * silu).astype(o_ref.dtype)
        return carry

    lax.fori_loop(0, n_chunks, chunk_out, 0)


def _retention(proj, cos, sin, dec, gn, batch, seq):
    m = proj.shape[0]
    n_chunks = seq // RET_CHUNK
    col = lambda off: (lambda b, h: (b, off // LANES + h))
    return pl.pallas_call(
        functools.partial(_ret_kernel, chunk=RET_CHUNK),
        grid=(batch, RET_HEADS),
        in_specs=[
            pl.BlockSpec((seq, LANES), col(OFF_RQ)),
            pl.BlockSpec((seq, LANES), col(OFF_RK)),
            pl.BlockSpec((seq, LANES), col(OFF_RV)),
            pl.BlockSpec((seq, LANES), col(OFF_RG)),
            pl.BlockSpec((seq, LANES), lambda b, h: (0, 0)),
            pl.BlockSpec((seq, LANES), lambda b, h: (0, 0)),
            pl.BlockSpec((1, 2, RET_CHUNK), lambda b, h: (h, 0, 0)),
            pl.BlockSpec((1, LANES), lambda b, h: (0, h)),
        ],
        out_specs=pl.BlockSpec((seq, LANES), lambda b, h: (b, h)),
        out_shape=jax.ShapeDtypeStruct((m, RET_WIDTH), BF16),
        scratch_shapes=[
            pltpu.VMEM((n_chunks, HEAD_DIM, 2 * LANES), F32),
            pltpu.VMEM((n_chunks, HEAD_DIM, 2 * LANES), BF16),
        ],
        compiler_params=_params("parallel", "parallel"),
        name="retention",
    )(proj, proj, proj, proj, cos, sin, dec, gn)


def _mla_prep_kernel(cq_ref, kr_ref, ckv_ref, gqa_ref, gkva_ref, wuq_ref, wukv_ref, gq_ref, gkn_ref, gkr_ref,
                     cos_ref, sin_ref, q_out, k_out, v_out):
    cqn = _rms(cq_ref[...].astype(F32), gqa_ref[...]).astype(BF16)
    qm = jnp.dot(cqn, wuq_ref[...], preferred_element_type=F32)
    ckvn = _rms(ckv_ref[...].astype(F32), gkva_ref[...]).astype(BF16)
    kvm = jnp.dot(ckvn, wukv_ref[...], preferred_element_type=F32)
    cs = cos_ref[...]
    sn = sin_ref[...]
    kr = kr_ref[...].astype(F32)
    kr_ss = jnp.sum(kr * kr, axis=-1, keepdims=True)
    gq_n = gq_ref[:, :LANES]
    gq_r = gq_ref[:, LANES:]
    gk_n = gkn_ref[...]
    gk_r = gkr_ref[...]
    inv_d = 1.0 / MLA_QK
    for h in range(MLA_HEADS):
        o = MLA_PAD * h
        qn = qm[:, o:o + LANES]
        qr = qm[:, o + LANES:o + MLA_PAD]
        rq = lax.rsqrt((jnp.sum(qn * qn, axis=-1, keepdims=True) + jnp.sum(qr * qr, axis=-1, keepdims=True)) * inv_d
                       + EPS)
        q_out[:, o:o + LANES] = (qn * rq * gq_n).astype(BF16)
        q_out[:, o + LANES:o + MLA_PAD] = _rope(qr * rq * gq_r, cs, sn).astype(BF16)
        kn = kvm[:, o:o + LANES]
        rk = lax.rsqrt((jnp.sum(kn * kn, axis=-1, keepdims=True) + kr_ss) * inv_d + EPS)
        k_out[:, o:o + LANES] = (kn * rk * gk_n).astype(BF16)
        k_out[:, o + LANES:o + MLA_PAD] = _rope(kr * rk * gk_r, cs, sn).astype(BF16)
        v_out[:, LANES * h:LANES * (h + 1)] = kvm[:, o + LANES:o + MLA_PAD].astype(BF16)


def _mla_prep(proj, gqa, gkva, wuq, wukv, gq, gkn, gkr, cos, sin, seq):
    m = proj.shape[0]
    tm = TM_PREP
    nseq = seq // tm
    full = lambda shape: pl.BlockSpec(shape, lambda i: (0, 0))
    return pl.pallas_call(
        _mla_prep_kernel,
        grid=(m // tm,),
        in_specs=[
            pl.BlockSpec((tm, MLA_Q_RANK), lambda i: (i, OFF_CQ // MLA_Q_RANK)),
            pl.BlockSpec((tm, LANES), lambda i: (i, OFF_KROPE // LANES)),
            pl.BlockSpec((tm, MLA_KV_RANK), lambda i: (i, OFF_CKV // MLA_KV_RANK)),
            full((1, MLA_Q_RANK)),
            full((1, MLA_KV_RANK)),
            full((MLA_Q_RANK, MLA_HEADS * MLA_PAD)),
            full((MLA_KV_RANK, MLA_HEADS * MLA_PAD)),
            full((1, MLA_PAD)),
            full((1, LANES)),
            full((1, LANES)),
            pl.BlockSpec((tm, LANES), lambda i: (i % nseq, 0)),
            pl.BlockSpec((tm, LANES), lambda i: (i % nseq, 0)),
        ],
        out_specs=[
            pl.BlockSpec((tm, MLA_HEADS * MLA_PAD), lambda i: (i, 0)),
            pl.BlockSpec((tm, MLA_HEADS * MLA_PAD), lambda i: (i, 0)),
            pl.BlockSpec((tm, MLA_HEADS * MLA_V), lambda i: (i, 0)),
        ],
        out_shape=[
            jax.ShapeDtypeStruct((m, MLA_HEADS * MLA_PAD), BF16),
            jax.ShapeDtypeStruct((m, MLA_HEADS * MLA_PAD), BF16),
            jax.ShapeDtypeStruct((m, MLA_HEADS * MLA_V), BF16),
        ],
        compiler_params=_params("parallel"),
        name="mla_prep",
    )(proj, proj, proj, gqa, gkva, wuq, wukv, gq, gkn, gkr, cos, sin)


def _gqa_prep_kernel(gq_ref, gk_ref, nq_ref, nk_ref, cos_ref, sin_ref, q_out, k_out):
    cs = cos_ref[...]
    sn = sin_ref[...]
    nq = nq_ref[...]
    nk = nk_ref[...]
    for h in range(GQA_HEADS):
        sl = slice(LANES * h, LANES * (h + 1))
        q_out[:, sl] = _rope(_rms(gq_ref[:, sl].astype(F32), nq), cs, sn).astype(BF16)
    for h in range(GQA_KV_HEADS):
        sl = slice(LANES * h, LANES * (h + 1))
        k_out[:, sl] = _rope(_rms(gk_ref[:, sl].astype(F32), nk), cs, sn).astype(BF16)


def _gqa_prep(proj, nq, nk, cos, sin, seq):
    m = proj.shape[0]
    tm = TM_PREP
    nseq = seq // tm
    wq = GQA_HEADS * HEAD_DIM
    wk = GQA_KV_HEADS * HEAD_DIM
    return pl.pallas_call(
        _gqa_prep_kernel,
        grid=(m // tm,),
        in_specs=[
            pl.BlockSpec((tm, wq), lambda i: (i, OFF_GQ // wq)),
            pl.BlockSpec((tm, wk), lambda i: (i, OFF_GK // wk)),
            pl.BlockSpec((1, LANES), lambda i: (0, 0)),
            pl.BlockSpec((1, LANES), lambda i: (0, 0)),
            pl.BlockSpec((tm, LANES), lambda i: (i % nseq, 0)),
            pl.BlockSpec((tm, LANES), lambda i: (i % nseq, 0)),
        ],
        out_specs=[
            pl.BlockSpec((tm, wq), lambda i: (i, 0)),
            pl.BlockSpec((tm, wk), lambda i: (i, 0)),
        ],
        out_shape=[
            jax.ShapeDtypeStruct((m, wq), BF16),
            jax.ShapeDtypeStruct((m, wk), BF16),
        ],
        compiler_params=_params("parallel"),
        name="gqa_prep",
    )(proj, proj, nq, nk, cos, sin)


def _attn_kernel(q_ref, k_ref, v_ref, o_ref):
    s = lax.dot_general(q_ref[...], k_ref[...], (((1,), (1,)), ((), ())), preferred_element_type=F32)
    m = jnp.max(s, axis=-1, keepdims=True)
    p = jnp.exp(s - m)
    l = jnp.sum(p, axis=-1, keepdims=True)
    o = jnp.dot(p.astype(BF16), v_ref[...], preferred_element_type=F32)
    o_ref[...] = (o / l).astype(o_ref.dtype)


def _attention(q, k, v, batch, seq, heads, group, d_qk, d_v, v_col0):
    m = q.shape[0]
    tq = TQ_ATTN
    nq = seq // tq
    return pl.pallas_call(
        _attn_kernel,
        grid=(batch, heads, nq),
        in_specs=[
            pl.BlockSpec((tq, d_qk), lambda b, h, i: (b * nq + i, h)),
            pl.BlockSpec((seq, d_qk), lambda b, h, i: (b, h // group)),
            pl.BlockSpec((seq, d_v), lambda b, h, i: (b, v_col0 + h // group)),
        ],
        out_specs=pl.BlockSpec((tq, d_v), lambda b, h, i: (b * nq + i, h)),
        out_shape=jax.ShapeDtypeStruct((m, heads * d_v), BF16),
        compiler_params=_params("parallel", "parallel", "parallel"),
        name="attention",
    )(q, k, v)


def _outproj_kernel(x_ref, a_ref, b_ref, c_ref, w_ref, o_ref):
    wa = a_ref.shape[1]
    wb = b_ref.shape[1]
    acc = x_ref[...]
    acc = acc + jnp.dot(a_ref[...], w_ref[0:wa, :], preferred_element_type=F32)
    acc = acc + jnp.dot(b_ref[...], w_ref[wa:wa + wb, :], preferred_element_type=F32)
    acc = acc + jnp.dot(c_ref[...], w_ref[wa + wb:, :], preferred_element_type=F32)
    o_ref[...] = acc


def _outproj(x, a, b, c, w):
    m = x.shape[0]
    tm = TM_PROJ
    row = lambda width: pl.BlockSpec((tm, width), lambda i: (i, 0))
    return pl.pallas_call(
        _outproj_kernel,
        grid=(m // tm,),
        in_specs=[
            row(D_MODEL), row(a.shape[1]), row(b.shape[1]), row(c.shape[1]),
            pl.BlockSpec((D_MODEL, D_MODEL), lambda i: (0, 0), pipeline_mode=pl.Buffered(1)),
        ],
        out_specs=row(D_MODEL),
        out_shape=jax.ShapeDtypeStruct((m, D_MODEL), F32),
        compiler_params=_params("parallel"),
        name="out_proj",
    )(x, a, b, c, w)


def _mlp_kernel(x_ref, g_ref, wu_ref, wd_ref, o_ref, h_scr):
    @pl.when(pl.program_id(1) == 0)
    def _():
        x = x_ref[...]
        h_scr[...] = _rms(x, g_ref[...]).astype(BF16)
        o_ref[...] = x

    u = jnp.dot(h_scr[...], wu_ref[...], preferred_element_type=F32)
    u = jnp.square(jnp.maximum(u, 0.0)).astype(BF16)
    o_ref[...] += jnp.dot(u, wd_ref[...], preferred_element_type=F32)


def _mlp(x, g, wu, wd):
    m = x.shape[0]
    tm = TM_PROJ
    tf = TF_MLP
    return pl.pallas_call(
        _mlp_kernel,
        grid=(m // tm, D_FF // tf),
        in_specs=[
            pl.BlockSpec((tm, D_MODEL), lambda i, f: (i, 0)),
            pl.BlockSpec((1, D_MODEL), lambda i, f: (0, 0)),
            pl.BlockSpec((D_MODEL, tf), lambda i, f: (0, f)),
            pl.BlockSpec((tf, D_MODEL), lambda i, f: (f, 0)),
        ],
        out_specs=pl.BlockSpec((tm, D_MODEL), lambda i, f: (i, 0)),
        out_shape=jax.ShapeDtypeStruct((m, D_MODEL), F32),
        scratch_shapes=[pltpu.VMEM((tm, D_MODEL), BF16)],
        compiler_params=_params("parallel", "arbitrary"),
        name="mlp",
    )(x, g, wu, wd)


def _rope_tables(seq):
    t = jnp.arange(seq, dtype=F32)

    def half_tables(ang):
        c, s = jnp.cos(ang), jnp.sin(ang)
        return jnp.concatenate([c, c], axis=-1), jnp.concatenate([-s, s], axis=-1)

    inv = 1.0 / (ROPE_THETA ** (jnp.arange(0, HEAD_DIM, 2, dtype=F32) / HEAD_DIM))
    ret = half_tables(t[:, None] * inv[None, :])

    row = jnp.repeat(jnp.arange(seq // GRID_W), GRID_W, total_repeat_length=seq).astype(F32)
    colp = (jnp.arange(seq) % GRID_W).astype(F32)

    def axial(dim):
        half = dim // 2
        iv = 1.0 / (ROPE_THETA ** (jnp.arange(0, half, 2, dtype=F32) / half))
        return jnp.concatenate([row[:, None] * iv[None, :], colp[:, None] * iv[None, :]], axis=-1)

    ax128 = half_tables(axial(HEAD_DIM))
    a64 = axial(MLA_ROPE)
    c, s = jnp.cos(a64), jnp.sin(a64)
    z = jnp.zeros_like(c)
    ax64 = (jnp.concatenate([c, z, c, z], axis=-1), jnp.concatenate([-s, z, s, z], axis=-1))
    return ret, ax128, ax64


def _pad_rope_cols(w):
    z = jnp.zeros(w.shape[:-1] + (MLA_ROPE // 2,), w.dtype)
    return jnp.concatenate([w[..., :MLA_ROPE // 2], z, w[..., MLA_ROPE // 2:], z], axis=-1)


def _layer_weights(l, ln1_w, w_in, ret_decay_fwd, ret_decay_bwd, ret_gn_w, mla_q_a_norm, mla_w_uq, mla_kv_a_norm,
                   mla_w_ukv, mla_q_norm, mla_k_norm, gqa_q_norm, gqa_k_norm, w_out, ln2_w, w_up, w_down):
    w = w_in[l]
    o = 0
    parts = {}
    for name, width in (("rq", 512), ("rk", 512), ("rv", 512), ("rg", 512), ("cq", MLA_Q_RANK), ("ckv", MLA_KV_RANK),
                        ("kr", MLA_ROPE), ("gq", 1024), ("gk", 256), ("gv", 256)):
        parts[name] = w[:, o:o + width]
        o += width
    w_in_p = jnp.concatenate(
        [parts["cq"], _pad_rope_cols(parts["kr"]), parts["ckv"], parts["gk"], parts["gq"], parts["rq"], parts["rk"],
         parts["rv"], parts["rg"], parts["gv"]], axis=1).astype(BF16)

    uq = mla_w_uq[l].reshape(MLA_Q_RANK, MLA_HEADS, MLA_QK)
    uq_p = jnp.concatenate([uq[..., :MLA_NOPE], _pad_rope_cols(uq[..., MLA_NOPE:])], axis=-1)
    uq_p = uq_p.reshape(MLA_Q_RANK, MLA_HEADS * MLA_PAD).astype(BF16)
    qn = mla_q_norm[l]
    gq = (jnp.concatenate([qn[:MLA_NOPE], _pad_rope_cols(qn[MLA_NOPE:])]) * (MLA_QK ** -0.5)).reshape(1, MLA_PAD)
    kn = mla_k_norm[l]
    lg = jnp.stack([jax.nn.log_sigmoid(ret_decay_fwd[l].astype(F32)), jax.nn.log_sigmoid(ret_decay_bwd[l].astype(F32))],
                   axis=1)
    return dict(
        ln1=ln1_w[l].reshape(1, D_MODEL),
        w_in=w_in_p,
        dec=jnp.broadcast_to(lg[:, :, None], (RET_HEADS, 2, RET_CHUNK)),
        gn=ret_gn_w[l].reshape(1, RET_WIDTH),
        gqa_n=mla_q_a_norm[l].reshape(1, MLA_Q_RANK),
        gkva_n=mla_kv_a_norm[l].reshape(1, MLA_KV_RANK),
        wuq=uq_p,
        wukv=mla_w_ukv[l].astype(BF16),
        gq=gq,
        gkn=kn[:MLA_NOPE].reshape(1, LANES),
        gkr=_pad_rope_cols(kn[MLA_NOPE:]).reshape(1, LANES),
        nq=(gqa_q_norm[l] * (HEAD_DIM ** -0.5)).reshape(1, LANES),
        nk=gqa_k_norm[l].reshape(1, LANES),
        w_out=w_out[l].astype(BF16),
        ln2=ln2_w[l].reshape(1, D_MODEL),
        w_up=w_up[l].astype(BF16),
        w_down=w_down[l].astype(BF16),
    )


def _layer(x, p, tables, batch, seq):
    (ret_c, ret_s), (ax_c, ax_s), (m_c, m_s) = tables
    proj = _inproj(x, p["ln1"], p["w_in"])
    ret = _retention(proj, ret_c, ret_s, p["dec"], p["gn"], batch, seq)
    qm, km, vm = _mla_prep(proj, p["gqa_n"], p["gkva_n"], p["wuq"], p["wukv"], p["gq"], p["gkn"], p["gkr"], m_c, m_s,
                           seq)
    mla = _attention(qm, km, vm, batch, seq, MLA_HEADS, 1, MLA_PAD, MLA_V, 0)
    qg, kg = _gqa_prep(proj, p["nq"], p["nk"], ax_c, ax_s, seq)
    gqa = _attention(qg, kg, proj, batch, seq, GQA_HEADS, GQA_GROUP, HEAD_DIM, HEAD_DIM, OFF_GV // LANES)
    x = _outproj(x, ret, mla, gqa, p["w_out"])
    return _mlp(x, p["ln2"], p["w_up"], p["w_down"])


def kernel(x_prompt, x_sample, ln1_w, w_in, ret_decay_fwd, ret_decay_bwd, ret_gn_w, mla_q_a_norm, mla_w_uq,
           mla_kv_a_norm, mla_w_ukv, mla_q_norm, mla_k_norm, gqa_q_norm, gqa_k_norm, w_out, ln2_w, w_up, w_down):
    depth = w_in.shape[0]
    layers = [
        _layer_weights(l, ln1_w, w_in, ret_decay_fwd, ret_decay_bwd, ret_gn_w, mla_q_a_norm, mla_w_uq, mla_kv_a_norm,
                       mla_w_ukv, mla_q_norm, mla_k_norm, gqa_q_norm, gqa_k_norm, w_out, ln2_w, w_up, w_down)
        for l in range(depth)
    ]

    def trunk(x):
        batch, seq, _ = x.shape
        tables = _rope_tables(seq)
        y = x.reshape(batch * seq, D_MODEL)
        for p in layers:
            y = _layer(y, p, tables, batch, seq)
        return y.reshape(batch, seq, D_MODEL)

    return trunk(x_prompt), trunk(x_sample)
```

```python
import functools

import jax
import jax.numpy as jnp
from jax import lax
from jax.experimental import pallas as pl
from jax.experimental.pallas import tpu as pltpu

D_MODEL = 2048
HEAD_DIM = 128
RET_HEADS = 4
RET_WIDTH = RET_HEADS * HEAD_DIM
MLA_HEADS = 4
MLA_Q_RANK = 384
MLA_KV_RANK = 256
MLA_NOPE = 128
MLA_ROPE = 64
MLA_QK = MLA_NOPE + MLA_ROPE
MLA_V = 128
GQA_HEADS = 8
GQA_KV_HEADS = 2
GQA_GROUP = GQA_HEADS // GQA_KV_HEADS
D_FF = 4 * D_MODEL
GRID_W = 64
ROPE_THETA = 10000.0
EPS = 1e-6
LOG2E = 1.4426950408889634

LANES = 128
MLA_PAD = 2 * LANES

OFF_RQ = 0
OFF_RK = 512
OFF_RV = 1024
OFF_RG = 1536
OFF_CQ = 2048
OFF_CKV = 2432
OFF_KROPE = 2688
OFF_GQ = 2816
OFF_GK = 3840
OFF_GV = 4096
IN_PAD = 4352
PREP_COL0 = OFF_CQ
PREP_WIDTH = OFF_GV - OFF_CQ

TM_PROJ = 512
TF_MLP = 1024
TM_PREP = 512
TQ_ATTN = 1024
RB_ATTN = 512
TK_ATTN = 1024
RET_CHUNK = 256
VMEM_LIMIT = 52 * 1024 * 1024

F32 = jnp.float32
BF16 = jnp.bfloat16


def _params(*sem):
    return pltpu.CompilerParams(dimension_semantics=sem, vmem_limit_bytes=VMEM_LIMIT)


def _rms(x, g):
    ms = jnp.mean(x * x, axis=-1, keepdims=True)
    return x * lax.rsqrt(ms + EPS) * g


def _rope(x, c, s):
    return x * c + pltpu.roll(x, LANES // 2, 1) * s


def _inproj_kernel(x_ref, g_ref, w_ref, o_ref):
    h = _rms(x_ref[...], g_ref[...]).astype(BF16)
    n = o_ref.shape[1]
    for c0 in range(0, n, 1024):
        c1 = min(c0 + 1024, n)
        o_ref[:, c0:c1] = jnp.dot(h, w_ref[:, c0:c1], preferred_element_type=F32).astype(o_ref.dtype)


def _inproj(x, g, w):
    m = x.shape[0]
    return pl.pallas_call(
        _inproj_kernel,
        grid=(m // TM_PROJ,),
        in_specs=[
            pl.BlockSpec((TM_PROJ, D_MODEL), lambda i: (i, 0)),
            pl.BlockSpec((1, D_MODEL), lambda i: (0, 0)),
            pl.BlockSpec((D_MODEL, IN_PAD), lambda i: (0, 0), pipeline_mode=pl.Buffered(1)),
        ],
        out_specs=pl.BlockSpec((TM_PROJ, IN_PAD), lambda i: (i, 0)),
        out_shape=jax.ShapeDtypeStruct((m, IN_PAD), BF16),
        compiler_params=_params("parallel"),
        name="in_proj",
    )(x, g, w)


def _ret_kernel(q_ref, k_ref, v_ref, g_ref, cos_ref, sin_ref, dec_ref, gn_ref, o_ref, kv_scr, st_scr, *, chunk):
    seq = q_ref.shape[0]
    n_chunks = seq // chunk
    lgf = dec_ref[0, 0:1, :]
    lgb = dec_ref[0, 1:2, :]
    lgf_l = lgf[:, :LANES]
    lgb_l = lgb[:, :LANES]
    pos = lax.broadcasted_iota(jnp.int32, (chunk, LANES), 0).astype(F32)
    zeta_f = jnp.exp((chunk - 1.0 - pos) * lgf_l)
    zeta_b = jnp.exp(pos * lgb_l)
    inner_f = jnp.exp((pos + 1.0) * lgf_l)
    inner_b = jnp.exp((chunk - pos) * lgb_l)
    ii = lax.broadcasted_iota(jnp.int32, (chunk, chunk), 0)
    jj = lax.broadcasted_iota(jnp.int32, (chunk, chunk), 1)
    d = (ii - jj).astype(F32)
    dmask = jnp.where(d >= 0, jnp.exp(jnp.maximum(d, 0.0) * lgf), jnp.exp(jnp.maximum(-d, 0.0) * lgb))
    k_scale = HEAD_DIM ** -0.5

    def rows(c):
        return pl.ds(pl.multiple_of(c * chunk, chunk), chunk)

    def chunk_kv(c, carry):
        r = rows(c)
        k = _rope(k_ref[r, :].astype(F32), cos_ref[r, :], sin_ref[r, :]) * k_scale
        v = v_ref[r, :].astype(F32)
        vz = jnp.concatenate([v * zeta_f, v * zeta_b], axis=1).astype(BF16)
        kt = k.T.astype(BF16)
        kv_scr[c] = jnp.dot(kt, vz, preferred_element_type=F32)
        return carry

    lax.fori_loop(0, n_chunks, chunk_kv, 0)

    dec_f = jnp.exp(chunk * lgf_l)
    dec_b = jnp.exp(chunk * lgb_l)
    sf = jnp.zeros((HEAD_DIM, LANES), F32)
    for c in range(n_chunks):
        st_scr[c, :, :LANES] = sf.astype(BF16)
        sf = dec_f * sf + kv_scr[c, :, :LANES]
    sb = jnp.zeros((HEAD_DIM, LANES), F32)
    for c in range(n_chunks - 1, -1, -1):
        st_scr[c, :, LANES:] = sb.astype(BF16)
        sb = dec_b * sb + kv_scr[c, :, LANES:]

    gn = gn_ref[...]

    def chunk_out(c, carry):
        r = rows(c)
        cs = cos_ref[r, :]
        sn = sin_ref[r, :]
        q = _rope(q_ref[r, :].astype(F32), cs, sn).astype(BF16)
        k = (_rope(k_ref[r, :].astype(F32), cs, sn) * k_scale).astype(BF16)
        v = v_ref[r, :].astype(BF16)
        a = lax.dot_general(q, k, (((1,), (1,)), ((), ())), preferred_element_type=F32) * dmask
        y = jnp.dot(a.astype(BF16), v, preferred_element_type=F32)
        cr = jnp.dot(q, st_scr[c], preferred_element_type=F32)
        y = y + cr[:, :LANES] * inner_f + cr[:, LANES:] * inner_b
        mu = jnp.mean(y, axis=-1, keepdims=True)
        yc = y - mu
        var = jnp.mean(yc * yc, axis=-1, keepdims=True)
        yn = yc * lax.rsqrt(var + EPS)
        g = g_ref[r, :].astype(F32)
        silu = g * (1.0 / (1.0 + jnp.exp(-g)))
        o_ref[r, :] = (yn * gn * silu).astype(o_ref.dtype)
        return carry

    lax.fori_loop(0, n_chunks, chunk_out, 0)


def _retention(proj, cos, sin, dec, gn, batch, seq):
    m = proj.shape[0]
    n_chunks = seq // RET_CHUNK
    col = lambda off: (lambda b, h: (b, off // LANES + h))
    return pl.pallas_call(
        functools.partial(_ret_kernel, chunk=RET_CHUNK),
        grid=(batch, RET_HEADS),
        in_specs=[
            pl.BlockSpec((seq, LANES), col(OFF_RQ)),
            pl.BlockSpec((seq, LANES), col(OFF_RK)),
            pl.BlockSpec((seq, LANES), col(OFF_RV)),
            pl.BlockSpec((seq, LANES), col(OFF_RG)),
            pl.BlockSpec((seq, LANES), lambda b, h: (0, 0)),
            pl.BlockSpec((seq, LANES), lambda b, h: (0, 0)),
            pl.BlockSpec((1, 2, RET_CHUNK), lambda b, h: (h, 0, 0)),
            pl.BlockSpec((1, LANES), lambda b, h: (0, h)),
        ],
        out_specs=pl.BlockSpec((seq, LANES), lambda b, h: (b, h)),
        out_shape=jax.ShapeDtypeStruct((m, RET_WIDTH), BF16),
        scratch_shapes=[
            pltpu.VMEM((n_chunks, HEAD_DIM, 2 * LANES), F32),
            pltpu.VMEM((n_chunks, HEAD_DIM, 2 * LANES), BF16),
        ],
        compiler_params=_params("parallel", "parallel"),
        name="retention",
    )(proj, proj, proj, proj, cos, sin, dec, gn)


def _prep_kernel(x_ref, gqa_ref, gkva_ref, wuq_ref, wukv_ref, gq_ref, gkn_ref, gkr_ref, nq_ref, nk_ref,
                 mcos_ref, msin_ref, acos_ref, asin_ref, qm_out, km_out, vm_out, qg_out, kg_out):
    def cols(off, width):
        return x_ref[:, off - PREP_COL0:off - PREP_COL0 + width].astype(F32)

    cqn = _rms(cols(OFF_CQ, MLA_Q_RANK), gqa_ref[...]).astype(BF16)
    qm = jnp.dot(cqn, wuq_ref[...], preferred_element_type=F32)
    ckvn = _rms(cols(OFF_CKV, MLA_KV_RANK), gkva_ref[...]).astype(BF16)
    kvm = jnp.dot(ckvn, wukv_ref[...], preferred_element_type=F32)
    cs = mcos_ref[...]
    sn = msin_ref[...]
    kr = cols(OFF_KROPE, LANES)
    kr_ss = jnp.sum(kr * kr, axis=-1, keepdims=True)
    gq_n = gq_ref[:, :LANES]
    gq_r = gq_ref[:, LANES:]
    gk_n = gkn_ref[...]
    gk_r = gkr_ref[...]
    inv_d = 1.0 / MLA_QK
    for h in range(MLA_HEADS):
        o = MLA_PAD * h
        qn = qm[:, o:o + LANES]
        qr = qm[:, o + LANES:o + MLA_PAD]
        rq = lax.rsqrt((jnp.sum(qn * qn, axis=-1, keepdims=True) + jnp.sum(qr * qr, axis=-1, keepdims=True)) * inv_d
                       + EPS)
        qm_out[:, o:o + LANES] = (qn * rq * gq_n).astype(BF16)
        qm_out[:, o + LANES:o + MLA_PAD] = _rope(qr * rq * gq_r, cs, sn).astype(BF16)
        kn = kvm[:, o:o + LANES]
        rk = lax.rsqrt((jnp.sum(kn * kn, axis=-1, keepdims=True) + kr_ss) * inv_d + EPS)
        km_out[:, o:o + LANES] = (kn * rk * gk_n).astype(BF16)
        km_out[:, o + LANES:o + MLA_PAD] = _rope(kr * rk * gk_r, cs, sn).astype(BF16)
        vm_out[:, LANES * h:LANES * (h + 1)] = kvm[:, o + LANES:o + MLA_PAD].astype(BF16)

    cs = acos_ref[...]
    sn = asin_ref[...]
    nq = nq_ref[...]
    nk = nk_ref[...]
    for h in range(GQA_HEADS):
        qg_out[:, LANES * h:LANES * (h + 1)] = _rope(_rms(cols(OFF_GQ + LANES * h, LANES), nq), cs, sn).astype(BF16)
    for h in range(GQA_KV_HEADS):
        kg_out[:, LANES * h:LANES * (h + 1)] = _rope(_rms(cols(OFF_GK + LANES * h, LANES), nk), cs, sn).astype(BF16)


def _prep(proj, p, mcos, msin, acos, asin, seq):
    m = proj.shape[0]
    tm = TM_PREP
    nseq = seq // tm
    full = lambda shape: pl.BlockSpec(shape, lambda i: (0, 0))
    table = pl.BlockSpec((tm, LANES), lambda i: (i % nseq, 0))
    widths = (MLA_HEADS * MLA_PAD, MLA_HEADS * MLA_PAD, MLA_HEADS * MLA_V, GQA_HEADS * HEAD_DIM,
              GQA_KV_HEADS * HEAD_DIM)
    return pl.pallas_call(
        _prep_kernel,
        grid=(m // tm,),
        in_specs=[
            pl.BlockSpec((tm, PREP_WIDTH), lambda i: (i, PREP_COL0 // PREP_WIDTH)),
            full((1, MLA_Q_RANK)),
            full((1, MLA_KV_RANK)),
            full((MLA_Q_RANK, MLA_HEADS * MLA_PAD)),
            full((MLA_KV_RANK, MLA_HEADS * MLA_PAD)),
            full((1, MLA_PAD)),
            full((1, LANES)),
            full((1, LANES)),
            full((1, LANES)),
            full((1, LANES)),
            table, table, table, table,
        ],
        out_specs=[pl.BlockSpec((tm, w), lambda i: (i, 0)) for w in widths],
        out_shape=[jax.ShapeDtypeStruct((m, w), BF16) for w in widths],
        compiler_params=_params("parallel"),
        name="qkv_prep",
    )(proj, p["gqa_n"], p["gkva_n"], p["wuq"], p["wukv"], p["gq"], p["gkn"], p["gkr"], p["nq"], p["nk"],
      mcos, msin, acos, asin)


def _attn_kernel(q_ref, k_ref, v_ref, o_ref, *, tk, rb):
    seq = k_ref.shape[0]
    nr = q_ref.shape[0] // rb
    for r in range(nr):
        rows = slice(r * rb, (r + 1) * rb)
        q = q_ref[rows, :]
        m = l = acc = None
        for j in range(seq // tk):
            ks = slice(j * tk, (j + 1) * tk)
            s = lax.dot_general(q, k_ref[ks, :], (((1,), (1,)), ((), ())), preferred_element_type=F32)
            mj = jnp.max(s, axis=-1, keepdims=True)
            if j == 0:
                m = mj
                p = jnp.exp2(s - m)
                l = jnp.sum(p, axis=-1, keepdims=True)
                acc = jnp.dot(p.astype(BF16), v_ref[ks, :], preferred_element_type=F32)
            else:
                m_new = jnp.maximum(m, mj)
                alpha = jnp.exp2(m - m_new)
                p = jnp.exp2(s - m_new)
                l = alpha * l + jnp.sum(p, axis=-1, keepdims=True)
                acc = alpha * acc + jnp.dot(p.astype(BF16), v_ref[ks, :], preferred_element_type=F32)
                m = m_new
        o_ref[rows, :] = (acc / l).astype(o_ref.dtype)


def _attention(q, k, v, batch, seq, heads, group, d_qk, d_v, v_col0):
    m = q.shape[0]
    tq = TQ_ATTN
    nq = seq // tq
    return pl.pallas_call(
        functools.partial(_attn_kernel, tk=TK_ATTN, rb=RB_ATTN),
        grid=(batch, heads, nq),
        in_specs=[
            pl.BlockSpec((tq, d_qk), lambda b, h, i: (b * nq + i, h)),
            pl.BlockSpec((seq, d_qk), lambda b, h, i: (b, h // group)),
            pl.BlockSpec((seq, d_v), lambda b, h, i: (b, v_col0 + h // group)),
        ],
        out_specs=pl.BlockSpec((tq, d_v), lambda b, h, i: (b * nq + i, h)),
        out_shape=jax.ShapeDtypeStruct((m, heads * d_v), BF16),
        compiler_params=_params("parallel", "parallel", "parallel"),
        name="attention",
    )(q, k, v)


def _outproj_kernel(x_ref, a_ref, b_ref, c_ref, w_ref, o_ref):
    wa = a_ref.shape[1]
    wb = b_ref.shape[1]
    acc = x_ref[...]
    acc = acc + jnp.dot(a_ref[...], w_ref[0:wa, :], preferred_element_type=F32)
    acc = acc + jnp.dot(b_ref[...], w_ref[wa:wa + wb, :], preferred_element_type=F32)
    acc = acc + jnp.dot(c_ref[...], w_ref[wa + wb:, :], preferred_element_type=F32)
    o_ref[...] = acc


def _outproj(x, a, b, c, w):
    m = x.shape[0]
    tm = TM_PROJ
    row = lambda width: pl.BlockSpec((tm, width), lambda i: (i, 0))
    return pl.pallas_call(
        _outproj_kernel,
        grid=(m // tm,),
        in_specs=[
            row(D_MODEL), row(a.shape[1]), row(b.shape[1]), row(c.shape[1]),
            pl.BlockSpec((D_MODEL, D_MODEL), lambda i: (0, 0), pipeline_mode=pl.Buffered(1)),
        ],
        out_specs=row(D_MODEL),
        out_shape=jax.ShapeDtypeStruct((m, D_MODEL), F32),
        compiler_params=_params("parallel"),
        name="out_proj",
    )(x, a, b, c, w)


def _mlp_kernel(x_ref, g_ref, wu_ref, wd_ref, o_ref, h_scr):
    @pl.when(pl.program_id(1) == 0)
    def _():
        x = x_ref[...]
        h_scr[...] = _rms(x, g_ref[...]).astype(BF16)
        o_ref[...] = x

    u = jnp.dot(h_scr[...], wu_ref[...], preferred_element_type=F32)
    u = jnp.square(jnp.maximum(u, 0.0)).astype(BF16)
    o_ref[...] += jnp.dot(u, wd_ref[...], preferred_element_type=F32)


def _mlp(x, g, wu, wd):
    m = x.shape[0]
    tm = TM_PROJ
    tf = TF_MLP
    return pl.pallas_call(
        _mlp_kernel,
        grid=(m // tm, D_FF // tf),
        in_specs=[
            pl.BlockSpec((tm, D_MODEL), lambda i, f: (i, 0)),
            pl.BlockSpec((1, D_MODEL), lambda i, f: (0, 0)),
            pl.BlockSpec((D_MODEL, tf), lambda i, f: (0, f)),
            pl.BlockSpec((tf, D_MODEL), lambda i, f: (f, 0)),
        ],
        out_specs=pl.BlockSpec((tm, D_MODEL), lambda i, f: (i, 0)),
        out_shape=jax.ShapeDtypeStruct((m, D_MODEL), F32),
        scratch_shapes=[pltpu.VMEM((tm, D_MODEL), BF16)],
        compiler_params=_params("parallel", "arbitrary"),
        name="mlp",
    )(x, g, wu, wd)


def _rope_tables(seq):
    t = jnp.arange(seq, dtype=F32)

    def half_tables(ang):
        c, s = jnp.cos(ang), jnp.sin(ang)
        return jnp.concatenate([c, c], axis=-1), jnp.concatenate([-s, s], axis=-1)

    inv = 1.0 / (ROPE_THETA ** (jnp.arange(0, HEAD_DIM, 2, dtype=F32) / HEAD_DIM))
    ret = half_tables(t[:, None] * inv[None, :])

    row = jnp.repeat(jnp.arange(seq // GRID_W), GRID_W, total_repeat_length=seq).astype(F32)
    colp = (jnp.arange(seq) % GRID_W).astype(F32)

    def axial(dim):
        half = dim // 2
        iv = 1.0 / (ROPE_THETA ** (jnp.arange(0, half, 2, dtype=F32) / half))
        return jnp.concatenate([row[:, None] * iv[None, :], colp[:, None] * iv[None, :]], axis=-1)

    ax128 = half_tables(axial(HEAD_DIM))
    a64 = axial(MLA_ROPE)
    c, s = jnp.cos(a64), jnp.sin(a64)
    z = jnp.zeros_like(c)
    ax64 = (jnp.concatenate([c, z, c, z], axis=-1), jnp.concatenate([-s, z, s, z], axis=-1))
    return ret, ax128, ax64


def _pad_rope_cols(w):
    z = jnp.zeros(w.shape[:-1] + (MLA_ROPE // 2,), w.dtype)
    return jnp.concatenate([w[..., :MLA_ROPE // 2], z, w[..., MLA_ROPE // 2:], z], axis=-1)


def _layer_weights(l, ln1_w, w_in, ret_decay_fwd, ret_decay_bwd, ret_gn_w, mla_q_a_norm, mla_w_uq, mla_kv_a_norm,
                   mla_w_ukv, mla_q_norm, mla_k_norm, gqa_q_norm, gqa_k_norm, w_out, ln2_w, w_up, w_down):
    w = w_in[l].astype(BF16)
    kr0 = OFF_KROPE
    w_in_p = jnp.concatenate([w[:, :kr0], _pad_rope_cols(w[:, kr0:kr0 + MLA_ROPE]), w[:, kr0 + MLA_ROPE:]], axis=1)

    uq = mla_w_uq[l].reshape(MLA_Q_RANK, MLA_HEADS, MLA_QK)
    uq_p = jnp.concatenate([uq[..., :MLA_NOPE], _pad_rope_cols(uq[..., MLA_NOPE:])], axis=-1)
    uq_p = uq_p.reshape(MLA_Q_RANK, MLA_HEADS * MLA_PAD).astype(BF16)
    qn = mla_q_norm[l]
    gq = (jnp.concatenate([qn[:MLA_NOPE], _pad_rope_cols(qn[MLA_NOPE:])]) * (MLA_QK ** -0.5 * LOG2E)).reshape(1, MLA_PAD)
    kn = mla_k_norm[l]
    lg = jnp.stack([jax.nn.log_sigmoid(ret_decay_fwd[l].astype(F32)), jax.nn.log_sigmoid(ret_decay_bwd[l].astype(F32))],
                   axis=1)
    return dict(
        ln1=ln1_w[l].reshape(1, D_MODEL),
        w_in=w_in_p,
        dec=jnp.broadcast_to(lg[:, :, None], (RET_HEADS, 2, RET_CHUNK)),
        gn=ret_gn_w[l].reshape(1, RET_WIDTH),
        gqa_n=mla_q_a_norm[l].reshape(1, MLA_Q_RANK),
        gkva_n=mla_kv_a_norm[l].reshape(1, MLA_KV_RANK),
        wuq=uq_p,
        wukv=mla_w_ukv[l].astype(BF16),
        gq=gq,
        gkn=kn[:MLA_NOPE].reshape(1, LANES),
        gkr=_pad_rope_cols(kn[MLA_NOPE:]).reshape(1, LANES),
        nq=(gqa_q_norm[l] * (HEAD_DIM ** -0.5 * LOG2E)).reshape(1, LANES),
        nk=gqa_k_norm[l].reshape(1, LANES),
        w_out=w_out[l].astype(BF16),
        ln2=ln2_w[l].reshape(1, D_MODEL),
        w_up=w_up[l].astype(BF16),
        w_down=w_down[l].astype(BF16),
    )


def _layer(x, p, tables, batch, seq):
    (ret_c, ret_s), (ax_c, ax_s), (m_c, m_s) = tables
    proj = _inproj(x, p["ln1"], p["w_in"])
    ret = _retention(proj, ret_c, ret_s, p["dec"], p["gn"], batch, seq)
    qm, km, vm, qg, kg = _prep(proj, p, m_c, m_s, ax_c, ax_s, seq)
    mla = _attention(qm, km, vm, batch, seq, MLA_HEADS, 1, MLA_PAD, MLA_V, 0)
    gqa = _attention(qg, kg, proj, batch, seq, GQA_HEADS, GQA_GROUP, HEAD_DIM, HEAD_DIM, OFF_GV // LANES)
    x = _outproj(x, ret, mla, gqa, p["w_out"])
    return _mlp(x, p["ln2"], p["w_up"], p["w_down"])


def kernel(x_prompt, x_sample, ln1_w, w_in, ret_decay_fwd, ret_decay_bwd, ret_gn_w, mla_q_a_norm, mla_w_uq,
           mla_kv_a_norm, mla_w_ukv, mla_q_norm, mla_k_norm, gqa_q_norm, gqa_k_norm, w_out, ln2_w, w_up, w_down):
    depth = w_in.shape[0]
    layers = [
        _layer_weights(l, ln1_w, w_in, ret_decay_fwd, ret_decay_bwd, ret_gn_w, mla_q_a_norm, mla_w_uq, mla_kv_a_norm,
                       mla_w_ukv, mla_q_norm, mla_k_norm, gqa_q_norm, gqa_k_norm, w_out, ln2_w, w_up, w_down)
        for l in range(depth)
    ]

    def trunk(x):
        batch, seq, _ = x.shape
        tables = _rope_tables(seq)
        y = x.reshape(batch * seq, D_MODEL)
        for p in layers:
            y = _layer(y, p, tables, batch, seq)
        return y.reshape(batch, seq, D_MODEL)

    return trunk(x_prompt), trunk(x_sample)
```

```python
import functools

import jax
import jax.numpy as jnp
from jax import lax
from jax.experimental import pallas as pl
from jax.experimental.pallas import tpu as pltpu

D_MODEL = 2048
HEAD_DIM = 128
RET_HEADS = 4
RET_WIDTH = RET_HEADS * HEAD_DIM
MLA_HEADS = 4
MLA_Q_RANK = 384
MLA_KV_RANK = 256
MLA_NOPE = 128
MLA_ROPE = 64
MLA_QK = MLA_NOPE + MLA_ROPE
MLA_V = 128
GQA_HEADS = 8
GQA_KV_HEADS = 2
GQA_GROUP = GQA_HEADS // GQA_KV_HEADS
D_FF = 4 * D_MODEL
GRID_W = 64
ROPE_THETA = 10000.0
EPS = 1e-6
LOG2E = 1.4426950408889634

LANES = 128
MLA_PAD = 2 * LANES

OFF_RQ = 0
OFF_RK = 512
OFF_RV = 1024
OFF_RG = 1536
OFF_CQ = 2048
OFF_CKV = 2432
OFF_KROPE = 2688
OFF_GQ = 2816
OFF_GK = 3840
OFF_GV = 4096
IN_PAD = 4352
V_AUG = 2 * LANES

TM_PROJ = 512
TM_MLP = 512
TF_MLP = 1024
TQ_ATTN = 2048
RB_ATTN = 512
TK_LONG_ATTN = 1024
TK_SHORT_ATTN = 256
RET_CHUNK = 256
VMEM_LIMIT = 52 * 1024 * 1024

F32 = jnp.float32
BF16 = jnp.bfloat16


def _params(*sem):
    return pltpu.CompilerParams(dimension_semantics=sem, vmem_limit_bytes=VMEM_LIMIT)


def _rms(x, g):
    ms = jnp.mean(x * x, axis=-1, keepdims=True)
    return x * lax.rsqrt(ms + EPS) * g


def _rope(x, c, s):
    return x * c + pltpu.roll(x, LANES // 2, 1) * s


def _inproj_kernel(x_ref, g_ref, w_ref, gqa_ref, gkva_ref, wuq_ref, wukv_ref, gq_ref, gkn_ref, gkr_ref, nq_ref, nk_ref,
                   mcos_ref, msin_ref, acos_ref, asin_ref, ret_out, qm_out, km_out, vm_out, qg_out, kg_out, vg_out):
    h = _rms(x_ref[...], g_ref[...]).astype(BF16)
    ones = jnp.ones((x_ref.shape[0], LANES), BF16)

    def proj(c0, c1):
        return jnp.dot(h, w_ref[:, c0:c1], preferred_element_type=F32)

    split = (OFF_CQ + OFF_GV) // 2
    pa = proj(OFF_CQ, split)
    pb = proj(split, OFF_GV)

    def cols(off, width):
        if off < split:
            return pa[:, off - OFF_CQ:off - OFF_CQ + width]
        return pb[:, off - split:off - split + width]

    cqn = _rms(cols(OFF_CQ, MLA_Q_RANK), gqa_ref[...]).astype(BF16)
    qm = jnp.dot(cqn, wuq_ref[...], preferred_element_type=F32)
    ckvn = _rms(cols(OFF_CKV, MLA_KV_RANK), gkva_ref[...]).astype(BF16)
    kvm = jnp.dot(ckvn, wukv_ref[...], preferred_element_type=F32)
    cs = mcos_ref[...]
    sn = msin_ref[...]
    kr = cols(OFF_KROPE, LANES)
    kr_ss = jnp.sum(kr * kr, axis=-1, keepdims=True)
    kr_rot = _rope(kr * gkr_ref[...], cs, sn)
    gq_n = gq_ref[:, :LANES]
    gq_r = gq_ref[:, LANES:]
    gk_n = gkn_ref[...]
    inv_d = 1.0 / MLA_QK
    for hd in range(MLA_HEADS):
        o = MLA_PAD * hd
        qn = qm[:, o:o + LANES]
        qr = qm[:, o + LANES:o + MLA_PAD]
        rq = lax.rsqrt((jnp.sum(qn * qn, axis=-1, keepdims=True) + jnp.sum(qr * qr, axis=-1, keepdims=True)) * inv_d
                       + EPS)
        qm_out[:, o:o + LANES] = (qn * rq * gq_n).astype(BF16)
        qm_out[:, o + LANES:o + MLA_PAD] = _rope(qr * rq * gq_r, cs, sn).astype(BF16)
        kn = kvm[:, o:o + LANES]
        rk = lax.rsqrt((jnp.sum(kn * kn, axis=-1, keepdims=True) + kr_ss) * inv_d + EPS)
        km_out[:, o:o + LANES] = (kn * rk * gk_n).astype(BF16)
        km_out[:, o + LANES:o + MLA_PAD] = (kr_rot * rk).astype(BF16)
        vm_out[:, o:o + LANES] = kvm[:, o + LANES:o + MLA_PAD].astype(BF16)
        vm_out[:, o + LANES:o + V_AUG] = ones

    cs = acos_ref[...]
    sn = asin_ref[...]
    nq = nq_ref[...]
    nk = nk_ref[...]
    for hd in range(GQA_HEADS):
        qg_out[:, LANES * hd:LANES * (hd + 1)] = _rope(_rms(cols(OFF_GQ + LANES * hd, LANES), nq), cs, sn).astype(BF16)
    for hd in range(GQA_KV_HEADS):
        kg_out[:, LANES * hd:LANES * (hd + 1)] = _rope(_rms(cols(OFF_GK + LANES * hd, LANES), nk), cs, sn).astype(BF16)

    gv = proj(OFF_GV, IN_PAD).astype(BF16)
    for hd in range(GQA_KV_HEADS):
        vg_out[:, V_AUG * hd:V_AUG * hd + LANES] = gv[:, LANES * hd:LANES * (hd + 1)]
        vg_out[:, V_AUG * hd + LANES:V_AUG * (hd + 1)] = ones

    for c0 in range(0, OFF_CQ, 1024):
        ret_out[:, c0:c0 + 1024] = proj(c0, c0 + 1024).astype(BF16)


def _inproj(x, w, p, mcos, msin, acos, asin, seq):
    m = x.shape[0]
    tm = TM_PROJ
    nseq = seq // tm
    l = p["l"]
    full = lambda shape: pl.BlockSpec(shape, lambda i: (0, 0))
    layer = lambda shape: pl.BlockSpec((None,) + shape, lambda i: (l, 0, 0))
    table = pl.BlockSpec((tm, LANES), lambda i: (i % nseq, 0))
    widths = (OFF_CQ, MLA_HEADS * MLA_PAD, MLA_HEADS * MLA_PAD, MLA_HEADS * V_AUG, GQA_HEADS * HEAD_DIM,
              GQA_KV_HEADS * HEAD_DIM, GQA_KV_HEADS * V_AUG)
    return pl.pallas_call(
        _inproj_kernel,
        grid=(m // tm,),
        in_specs=[
            pl.BlockSpec((tm, D_MODEL), lambda i: (i, 0)),
            full((1, D_MODEL)),
            pl.BlockSpec((None, D_MODEL, IN_PAD), lambda i: (l, 0, 0), pipeline_mode=pl.Buffered(1)),
            full((1, MLA_Q_RANK)),
            full((1, MLA_KV_RANK)),
            layer((MLA_Q_RANK, MLA_HEADS * MLA_PAD)),
            layer((MLA_KV_RANK, MLA_HEADS * MLA_PAD)),
            full((1, MLA_PAD)),
            full((1, LANES)),
            full((1, LANES)),
            full((1, LANES)),
            full((1, LANES)),
            table, table, table, table,
        ],
        out_specs=[pl.BlockSpec((tm, wd), lambda i: (i, 0)) for wd in widths],
        out_shape=[jax.ShapeDtypeStruct((m, wd), BF16) for wd in widths],
        compiler_params=_params("parallel"),
        name="in_proj",
    )(x, p["ln1"], w["w_in"], p["gqa_n"], p["gkva_n"], w["wuq"], w["wukv"], p["gq"], p["gkn"], p["gkr"], p["nq"],
      p["nk"], mcos, msin, acos, asin)


def _ret_kernel(q_ref, k_ref, v_ref, g_ref, cos_ref, sin_ref, dec_ref, gn_ref, o_ref, kv_scr, st_scr, k_scr, *, chunk):
    seq = q_ref.shape[0]
    n_chunks = seq // chunk
    lgf = dec_ref[0, 0:1, :]
    lgb = dec_ref[0, 1:2, :]
    lgf_l = lgf[:, :LANES]
    lgb_l = lgb[:, :LANES]
    k_scale = HEAD_DIM ** -0.5
    pos = lax.broadcasted_iota(jnp.int32, (chunk, LANES), 0).astype(F32)
    zeta_f = jnp.exp((chunk - 1.0 - pos) * lgf_l)
    zeta_b = jnp.exp(pos * lgb_l)
    inner_f = jnp.exp((pos + 1.0) * lgf_l) * k_scale
    inner_b = jnp.exp((chunk - pos) * lgb_l) * k_scale
    ii = lax.broadcasted_iota(jnp.int32, (chunk, chunk), 0)
    jj = lax.broadcasted_iota(jnp.int32, (chunk, chunk), 1)
    d = (ii - jj).astype(F32)
    dmask = jnp.where(d >= 0, jnp.exp(jnp.maximum(d, 0.0) * lgf), jnp.exp(jnp.maximum(-d, 0.0) * lgb)) * k_scale

    def rows(c):
        return slice(c * chunk, (c + 1) * chunk)

    def chunk_kv(c):
        r = rows(c)
        k = _rope(k_ref[r, :].astype(F32), cos_ref[r, :], sin_ref[r, :])
        k_scr[r, :] = k.astype(BF16)
        v = v_ref[r, :].astype(F32)
        vz = jnp.concatenate([v * zeta_f, v * zeta_b], axis=1).astype(BF16)
        kt = k.T.astype(BF16)
        kv_scr[c] = jnp.dot(kt, vz, preferred_element_type=F32)

    for c in range(n_chunks):
        chunk_kv(c)

    dec_f = jnp.exp(chunk * lgf_l)
    dec_b = jnp.exp(chunk * lgb_l)
    sf = jnp.zeros((HEAD_DIM, LANES), F32)
    sb = jnp.zeros((HEAD_DIM, LANES), F32)
    for c in range(n_chunks):
        cb = n_chunks - 1 - c
        st_scr[c, :, :LANES] = sf.astype(BF16)
        st_scr[cb, :, LANES:] = sb.astype(BF16)
        sf = dec_f * sf + kv_scr[c, :, :LANES]
        sb = dec_b * sb + kv_scr[cb, :, LANES:]

    gn = gn_ref[...]

    def chunk_out(c):
        r = rows(c)
        q = _rope(q_ref[r, :].astype(F32), cos_ref[r, :], sin_ref[r, :]).astype(BF16)
        a = lax.dot_general(q, k_scr[r, :], (((1,), (1,)), ((), ())), preferred_element_type=F32) * dmask
        y = jnp.dot(a.astype(BF16), v_ref[r, :].astype(BF16), preferred_element_type=F32)
        cr = jnp.dot(q, st_scr[c], preferred_element_type=F32)
        y = y + cr[:, :LANES] * inner_f + cr[:, LANES:] * inner_b
        mu = jnp.mean(y, axis=-1, keepdims=True)
        yc = y - mu
        var = jnp.mean(yc * yc, axis=-1, keepdims=True)
        yn = yc * lax.rsqrt(var + EPS)
        g = g_ref[r, :].astype(F32)
        silu = g * (1.0 / (1.0 + jnp.exp(-g)))
        o_ref[r, :] = (yn * gn * silu).astype(o_ref.dtype)

    for c in range(n_chunks):
        chunk_out(c)


def _retention(proj, cos, sin, dec, gn, batch, seq):
    m = proj.shape[0]
    n_chunks = seq // RET_CHUNK
    col = lambda off: (lambda b, h: (b, off // LANES + h))
    return pl.pallas_call(
        functools.partial(_ret_kernel, chunk=RET_CHUNK),
        grid=(batch, RET_HEADS),
        in_specs=[
            pl.BlockSpec((seq, LANES), col(OFF_RQ)),
            pl.BlockSpec((seq, LANES), col(OFF_RK)),
            pl.BlockSpec((seq, LANES), col(OFF_RV)),
            pl.BlockSpec((seq, LANES), col(OFF_RG)),
            pl.BlockSpec((seq, LANES), lambda b, h: (0, 0)),
            pl.BlockSpec((seq, LANES), lambda b, h: (0, 0)),
            pl.BlockSpec((1, 2, RET_CHUNK), lambda b, h: (h, 0, 0)),
            pl.BlockSpec((1, LANES), lambda b, h: (0, h)),
        ],
        out_specs=pl.BlockSpec((seq, LANES), lambda b, h: (b, h)),
        out_shape=jax.ShapeDtypeStruct((m, RET_WIDTH), BF16),
        scratch_shapes=[
            pltpu.VMEM((n_chunks, HEAD_DIM, 2 * LANES), F32),
            pltpu.VMEM((n_chunks, HEAD_DIM, 2 * LANES), BF16),
            pltpu.VMEM((seq, HEAD_DIM), BF16),
        ],
        compiler_params=_params("parallel", "parallel"),
        name="retention",
    )(proj, proj, proj, proj, cos, sin, dec, gn)


def _attn_kernel(q_ref, k_ref, v_ref, o_ref, *, tk, rb):
    seq = k_ref.shape[0]
    nr = q_ref.shape[0] // rb
    for r in range(nr):
        rows = slice(r * rb, (r + 1) * rb)
        q = q_ref[rows, :]
        m = acc = None
        for j in range(seq // tk):
            ks = slice(j * tk, (j + 1) * tk)
            s = lax.dot_general(q, k_ref[ks, :], (((1,), (1,)), ((), ())), preferred_element_type=F32)
            mj = jnp.max(s, axis=-1, keepdims=True)
            if j == 0:
                m = mj
                acc = jnp.dot(jnp.exp2(s - m).astype(BF16), v_ref[ks, :], preferred_element_type=F32)
            else:
                m_new = jnp.maximum(m, mj)
                alpha = jnp.exp2(m - m_new)
                acc = alpha * acc + jnp.dot(jnp.exp2(s - m_new).astype(BF16), v_ref[ks, :],
                                            preferred_element_type=F32)
                m = m_new
        o_ref[rows, :] = (acc[:, :LANES] / acc[:, LANES:]).astype(o_ref.dtype)


def _attention(q, k, v, batch, seq, heads, group, d_qk):
    m = q.shape[0]
    tq = TQ_ATTN
    nq = seq // tq
    d_v = HEAD_DIM
    return pl.pallas_call(
        functools.partial(_attn_kernel, tk=TK_LONG_ATTN if seq > TQ_ATTN else TK_SHORT_ATTN, rb=RB_ATTN),
        grid=(batch, heads, nq),
        in_specs=[
            pl.BlockSpec((tq, d_qk), lambda b, h, i: (b * nq + i, h)),
            pl.BlockSpec((seq, d_qk), lambda b, h, i: (b, h // group)),
            pl.BlockSpec((seq, V_AUG), lambda b, h, i: (b, h // group)),
        ],
        out_specs=pl.BlockSpec((tq, d_v), lambda b, h, i: (b * nq + i, h)),
        out_shape=jax.ShapeDtypeStruct((m, heads * d_v), BF16),
        compiler_params=_params("parallel", "parallel", "parallel"),
        name="attention",
    )(q, k, v)


def _outproj_kernel(x_ref, a_ref, b_ref, c_ref, w_ref, o_ref):
    wa = a_ref.shape[1]
    wb = b_ref.shape[1]
    acc = x_ref[...]
    acc = acc + jnp.dot(a_ref[...], w_ref[0:wa, :], preferred_element_type=F32)
    acc = acc + jnp.dot(b_ref[...], w_ref[wa:wa + wb, :], preferred_element_type=F32)
    acc = acc + jnp.dot(c_ref[...], w_ref[wa + wb:, :], preferred_element_type=F32)
    o_ref[...] = acc


def _outproj(x, a, b, c, w, l):
    m = x.shape[0]
    tm = TM_PROJ
    row = lambda width: pl.BlockSpec((tm, width), lambda i: (i, 0))
    return pl.pallas_call(
        _outproj_kernel,
        grid=(m // tm,),
        in_specs=[
            row(D_MODEL), row(a.shape[1]), row(b.shape[1]), row(c.shape[1]),
            pl.BlockSpec((None, D_MODEL, D_MODEL), lambda i: (l, 0, 0), pipeline_mode=pl.Buffered(1)),
        ],
        out_specs=row(D_MODEL),
        out_shape=jax.ShapeDtypeStruct((m, D_MODEL), F32),
        compiler_params=_params("parallel"),
        name="out_proj",
    )(x, a, b, c, w)


def _mlp_kernel(x_ref, g_ref, wu_ref, wd_ref, o_ref, h_scr):
    @pl.when(pl.program_id(1) == 0)
    def _():
        x = x_ref[...]
        h_scr[...] = _rms(x, g_ref[...]).astype(BF16)
        o_ref[...] = x

    u = jnp.dot(h_scr[...], wu_ref[...], preferred_element_type=F32)
    u = jnp.square(jnp.maximum(u, 0.0)).astype(BF16)
    o_ref[...] += jnp.dot(u, wd_ref[...], preferred_element_type=F32)


def _mlp(x, g, wu, wd, l):
    m = x.shape[0]
    tm = TM_MLP
    tf = TF_MLP
    return pl.pallas_call(
        _mlp_kernel,
        grid=(m // tm, D_FF // tf),
        in_specs=[
            pl.BlockSpec((tm, D_MODEL), lambda i, f: (i, 0)),
            pl.BlockSpec((1, D_MODEL), lambda i, f: (0, 0)),
            pl.BlockSpec((None, D_MODEL, tf), lambda i, f: (l, 0, f)),
            pl.BlockSpec((None, tf, D_MODEL), lambda i, f: (l, f, 0)),
        ],
        out_specs=pl.BlockSpec((tm, D_MODEL), lambda i, f: (i, 0)),
        out_shape=jax.ShapeDtypeStruct((m, D_MODEL), F32),
        scratch_shapes=[pltpu.VMEM((tm, D_MODEL), BF16)],
        compiler_params=_params("parallel", "arbitrary"),
        name="mlp",
    )(x, g, wu, wd)


def _rope_tables(seq):
    t = jnp.arange(seq, dtype=F32)

    def half_tables(ang):
        c, s = jnp.cos(ang), jnp.sin(ang)
        return jnp.concatenate([c, c], axis=-1), jnp.concatenate([-s, s], axis=-1)

    inv = 1.0 / (ROPE_THETA ** (jnp.arange(0, HEAD_DIM, 2, dtype=F32) / HEAD_DIM))
    ret = half_tables(t[:, None] * inv[None, :])

    ti = jnp.arange(seq)
    row = (ti // GRID_W).astype(F32)
    colp = (ti % GRID_W).astype(F32)

    def axial(dim):
        half = dim // 2
        iv = 1.0 / (ROPE_THETA ** (jnp.arange(0, half, 2, dtype=F32) / half))
        return jnp.concatenate([row[:, None] * iv[None, :], colp[:, None] * iv[None, :]], axis=-1)

    ax128 = half_tables(axial(HEAD_DIM))
    a64 = axial(MLA_ROPE)
    c, s = jnp.cos(a64), jnp.sin(a64)
    z = jnp.zeros_like(c)
    ax64 = (jnp.concatenate([c, z, c, z], axis=-1), jnp.concatenate([-s, z, s, z], axis=-1))
    return ret, ax128, ax64


def _pad_rope_cols(w):
    z = jnp.zeros(w.shape[:-1] + (MLA_ROPE // 2,), w.dtype)
    return jnp.concatenate([w[..., :MLA_ROPE // 2], z, w[..., MLA_ROPE // 2:], z], axis=-1)


def _matmul_weights(w_in, mla_w_uq, mla_w_ukv, w_out, w_up, w_down):
    depth = w_in.shape[0]
    kr0 = OFF_KROPE
    w_in_p = jnp.concatenate(
        [w_in[..., :kr0], _pad_rope_cols(w_in[..., kr0:kr0 + MLA_ROPE]), w_in[..., kr0 + MLA_ROPE:]], axis=-1)
    uq = mla_w_uq.reshape(depth, MLA_Q_RANK, MLA_HEADS, MLA_QK)
    uq_p = jnp.concatenate([uq[..., :MLA_NOPE], _pad_rope_cols(uq[..., MLA_NOPE:])], axis=-1)
    uq_p = uq_p.reshape(depth, MLA_Q_RANK, MLA_HEADS * MLA_PAD)
    return dict(w_in=w_in_p.astype(BF16), wuq=uq_p.astype(BF16), wukv=mla_w_ukv.astype(BF16),
                w_out=w_out.astype(BF16), w_up=w_up.astype(BF16), w_down=w_down.astype(BF16))


def _layer_params(l, ln1_w, ret_decay_fwd, ret_decay_bwd, ret_gn_w, mla_q_a_norm, mla_kv_a_norm, mla_q_norm,
                  mla_k_norm, gqa_q_norm, gqa_k_norm, ln2_w):
    qn = mla_q_norm[l]
    gq = (jnp.concatenate([qn[:MLA_NOPE], _pad_rope_cols(qn[MLA_NOPE:])]) * (MLA_QK ** -0.5 * LOG2E)).reshape(1, MLA_PAD)
    kn = mla_k_norm[l]
    lg = jnp.stack([jax.nn.log_sigmoid(ret_decay_fwd[l].astype(F32)), jax.nn.log_sigmoid(ret_decay_bwd[l].astype(F32))],
                   axis=1)
    return dict(
        l=l,
        ln1=ln1_w[l].reshape(1, D_MODEL),
        dec=jnp.broadcast_to(lg[:, :, None], (RET_HEADS, 2, RET_CHUNK)),
        gn=ret_gn_w[l].reshape(1, RET_WIDTH),
        gqa_n=mla_q_a_norm[l].reshape(1, MLA_Q_RANK),
        gkva_n=mla_kv_a_norm[l].reshape(1, MLA_KV_RANK),
        gq=gq,
        gkn=kn[:MLA_NOPE].reshape(1, LANES),
        gkr=_pad_rope_cols(kn[MLA_NOPE:]).reshape(1, LANES),
        nq=(gqa_q_norm[l] * (HEAD_DIM ** -0.5 * LOG2E)).reshape(1, LANES),
        nk=gqa_k_norm[l].reshape(1, LANES),
        ln2=ln2_w[l].reshape(1, D_MODEL),
    )


def _layer(x, w, p, tables, batch, seq):
    (ret_c, ret_s), (ax_c, ax_s), (m_c, m_s) = tables
    l = p["l"]
    rqkvg, qm, km, vm, qg, kg, vg = _inproj(x, w, p, m_c, m_s, ax_c, ax_s, seq)
    ret = _retention(rqkvg, ret_c, ret_s, p["dec"], p["gn"], batch, seq)
    mla = _attention(qm, km, vm, batch, seq, MLA_HEADS, 1, MLA_PAD)
    gqa = _attention(qg, kg, vg, batch, seq, GQA_HEADS, GQA_GROUP, HEAD_DIM)
    x = _outproj(x, ret, mla, gqa, w["w_out"], l)
    return _mlp(x, p["ln2"], w["w_up"], w["w_down"], l)


def kernel(x_prompt, x_sample, ln1_w, w_in, ret_decay_fwd, ret_decay_bwd, ret_gn_w, mla_q_a_norm, mla_w_uq,
           mla_kv_a_norm, mla_w_ukv, mla_q_norm, mla_k_norm, gqa_q_norm, gqa_k_norm, w_out, ln2_w, w_up, w_down):
    depth = w_in.shape[0]
    w = _matmul_weights(w_in, mla_w_uq, mla_w_ukv, w_out, w_up, w_down)
    layers = [
        _layer_params(l, ln1_w, ret_decay_fwd, ret_decay_bwd, ret_gn_w, mla_q_a_norm, mla_kv_a_norm, mla_q_norm,
                      mla_k_norm, gqa_q_norm, gqa_k_norm, ln2_w)
        for l in range(depth)
    ]

    def trunk(x):
        batch, seq, _ = x.shape
        tables = _rope_tables(seq)
        y = x.reshape(batch * seq, D_MODEL)
        for p in layers:
            y = _layer(y, w, p, tables, batch, seq)
        return y.reshape(batch, seq, D_MODEL)

    return trunk(x_prompt), trunk(x_sample)
```

```python
import functools

import jax
import jax.numpy as jnp
from jax import lax
from jax.experimental import pallas as pl
from jax.experimental.pallas import tpu as pltpu

D_MODEL = 2048
HEAD_DIM = 128
RET_HEADS = 4
RET_WIDTH = RET_HEADS * HEAD_DIM
MLA_HEADS = 4
MLA_Q_RANK = 384
MLA_KV_RANK = 256
MLA_NOPE = 128
MLA_ROPE = 64
MLA_QK = MLA_NOPE + MLA_ROPE
MLA_V = 128
GQA_HEADS = 8
GQA_KV_HEADS = 2
GQA_GROUP = GQA_HEADS // GQA_KV_HEADS
D_FF = 4 * D_MODEL
GRID_W = 64
ROPE_THETA = 10000.0
EPS = 1e-6
LOG2E = 1.4426950408889634

LANES = 128
MLA_PAD = 2 * LANES

OFF_RQ = 0
OFF_RK = 512
OFF_RV = 1024
OFF_RG = 1536
OFF_CQ = 2048
OFF_CKV = 2432
OFF_KROPE = 2688
OFF_GQ = 2816
OFF_GK = 3840
OFF_GV = 4096
IN_PAD = 4352
V_AUG = 2 * LANES

TM_PROJ = 512
TM_MLP = 512
TF_MLP = 1024
ROWS_ATTN = 4096
RB_ATTN = 512
TK_LONG_ATTN = 1024
TK_SHORT_ATTN = 256
RET_CHUNK = 256
VMEM_LIMIT = 52 * 1024 * 1024

F32 = jnp.float32
BF16 = jnp.bfloat16


def _params(*sem):
    return pltpu.CompilerParams(dimension_semantics=sem, vmem_limit_bytes=VMEM_LIMIT)


def _rms(x, g):
    ms = jnp.mean(x * x, axis=-1, keepdims=True)
    return x * lax.rsqrt(ms + EPS) * g


def _rope(x, c, s):
    return x * c + pltpu.roll(x, LANES // 2, 1) * s


def _inproj_kernel(x_ref, g_ref, w_ref, gqa_ref, gkva_ref, wuq_ref, wukv_ref, gq_ref, gkn_ref, gkr_ref, nq_ref, nk_ref,
                   mcos_ref, msin_ref, acos_ref, asin_ref, ret_out, qm_out, km_out, vm_out, qg_out, kg_out, vg_out):
    h = _rms(x_ref[...], g_ref[...]).astype(BF16)
    ones = jnp.ones((x_ref.shape[0], LANES), BF16)

    def proj(c0, c1):
        return jnp.dot(h, w_ref[:, c0:c1], preferred_element_type=F32)

    split = (OFF_CQ + OFF_GV) // 2
    pa = proj(OFF_CQ, split)
    pb = proj(split, OFF_GV)

    def cols(off, width):
        if off < split:
            return pa[:, off - OFF_CQ:off - OFF_CQ + width]
        return pb[:, off - split:off - split + width]

    cqn = _rms(cols(OFF_CQ, MLA_Q_RANK), gqa_ref[...]).astype(BF16)
    qm = jnp.dot(cqn, wuq_ref[...], preferred_element_type=F32)
    ckvn = _rms(cols(OFF_CKV, MLA_KV_RANK), gkva_ref[...]).astype(BF16)
    kvm = jnp.dot(ckvn, wukv_ref[...], preferred_element_type=F32)
    cs = mcos_ref[...]
    sn = msin_ref[...]
    kr = cols(OFF_KROPE, LANES)
    kr_ss = jnp.sum(kr * kr, axis=-1, keepdims=True)
    kr_rot = _rope(kr * gkr_ref[...], cs, sn)
    gq_n = gq_ref[:, :LANES]
    gq_r = gq_ref[:, LANES:]
    gk_n = gkn_ref[...]
    inv_d = 1.0 / MLA_QK
    for hd in range(MLA_HEADS):
        o = MLA_PAD * hd
        qn = qm[:, o:o + LANES]
        qr = qm[:, o + LANES:o + MLA_PAD]
        rq = lax.rsqrt((jnp.sum(qn * qn, axis=-1, keepdims=True) + jnp.sum(qr * qr, axis=-1, keepdims=True)) * inv_d
                       + EPS)
        qm_out[:, o:o + LANES] = (qn * rq * gq_n).astype(BF16)
        qm_out[:, o + LANES:o + MLA_PAD] = _rope(qr * rq * gq_r, cs, sn).astype(BF16)
        kn = kvm[:, o:o + LANES]
        rk = lax.rsqrt((jnp.sum(kn * kn, axis=-1, keepdims=True) + kr_ss) * inv_d + EPS)
        km_out[:, o:o + LANES] = (kn * rk * gk_n).astype(BF16)
        km_out[:, o + LANES:o + MLA_PAD] = (kr_rot * rk).astype(BF16)
        vm_out[:, o:o + LANES] = kvm[:, o + LANES:o + MLA_PAD].astype(BF16)
        vm_out[:, o + LANES:o + V_AUG] = ones

    cs = acos_ref[...]
    sn = asin_ref[...]
    nq = nq_ref[...]
    nk = nk_ref[...]
    for hd in range(GQA_HEADS):
        qg_out[:, LANES * hd:LANES * (hd + 1)] = _rope(_rms(cols(OFF_GQ + LANES * hd, LANES), nq), cs, sn).astype(BF16)
    for hd in range(GQA_KV_HEADS):
        kg_out[:, LANES * hd:LANES * (hd + 1)] = _rope(_rms(cols(OFF_GK + LANES * hd, LANES), nk), cs, sn).astype(BF16)

    gv = proj(OFF_GV, IN_PAD).astype(BF16)
    for hd in range(GQA_KV_HEADS):
        vg_out[:, V_AUG * hd:V_AUG * hd + LANES] = gv[:, LANES * hd:LANES * (hd + 1)]
        vg_out[:, V_AUG * hd + LANES:V_AUG * (hd + 1)] = ones

    for c0 in range(0, OFF_CQ, 1024):
        ret_out[:, c0:c0 + 1024] = proj(c0, c0 + 1024).astype(BF16)


def _inproj(x, w, p, mcos, msin, acos, asin, seq):
    m = x.shape[0]
    tm = TM_PROJ
    nseq = seq // tm
    l = p["l"]
    full = lambda shape: pl.BlockSpec(shape, lambda i: (0, 0))
    layer = lambda shape: pl.BlockSpec((None,) + shape, lambda i: (l, 0, 0))
    table = pl.BlockSpec((tm, LANES), lambda i: (i % nseq, 0))
    widths = (OFF_CQ, MLA_HEADS * MLA_PAD, MLA_HEADS * MLA_PAD, MLA_HEADS * V_AUG, GQA_HEADS * HEAD_DIM,
              GQA_KV_HEADS * HEAD_DIM, GQA_KV_HEADS * V_AUG)
    return pl.pallas_call(
        _inproj_kernel,
        grid=(m // tm,),
        in_specs=[
            pl.BlockSpec((tm, D_MODEL), lambda i: (i, 0)),
            full((1, D_MODEL)),
            pl.BlockSpec((None, D_MODEL, IN_PAD), lambda i: (l, 0, 0), pipeline_mode=pl.Buffered(1)),
            full((1, MLA_Q_RANK)),
            full((1, MLA_KV_RANK)),
            layer((MLA_Q_RANK, MLA_HEADS * MLA_PAD)),
            layer((MLA_KV_RANK, MLA_HEADS * MLA_PAD)),
            full((1, MLA_PAD)),
            full((1, LANES)),
            full((1, LANES)),
            full((1, LANES)),
            full((1, LANES)),
            table, table, table, table,
        ],
        out_specs=[pl.BlockSpec((tm, wd), lambda i: (i, 0)) for wd in widths],
        out_shape=[jax.ShapeDtypeStruct((m, wd), BF16) for wd in widths],
        compiler_params=_params("parallel"),
        name="in_proj",
    )(x, p["ln1"], w["w_in"], p["gqa_n"], p["gkva_n"], w["wuq"], w["wukv"], p["gq"], p["gkn"], p["gkr"], p["nq"],
      p["nk"], mcos, msin, acos, asin)


def _ret_kernel(q_ref, k_ref, v_ref, g_ref, cos_ref, sin_ref, dec_ref, gn_ref, o_ref, kv_scr, st_scr, k_scr, *, chunk):
    seq = q_ref.shape[0]
    n_chunks = seq // chunk
    lgf = dec_ref[0, 0:1, :]
    lgb = dec_ref[0, 1:2, :]
    lgf_l = lgf[:, :LANES]
    lgb_l = lgb[:, :LANES]
    k_scale = HEAD_DIM ** -0.5
    pos = lax.broadcasted_iota(jnp.int32, (chunk, LANES), 0).astype(F32)
    zeta_f = jnp.exp((chunk - 1.0 - pos) * lgf_l)
    zeta_b = jnp.exp(pos * lgb_l)
    inner_f = jnp.exp((pos + 1.0) * lgf_l) * k_scale
    inner_b = jnp.exp((chunk - pos) * lgb_l) * k_scale
    ii = lax.broadcasted_iota(jnp.int32, (chunk, chunk), 0)
    jj = lax.broadcasted_iota(jnp.int32, (chunk, chunk), 1)
    d = (ii - jj).astype(F32)
    dmask = jnp.where(d >= 0, jnp.exp(jnp.maximum(d, 0.0) * lgf), jnp.exp(jnp.maximum(-d, 0.0) * lgb)) * k_scale

    def rows(c):
        return slice(c * chunk, (c + 1) * chunk)

    def chunk_kv(c):
        r = rows(c)
        k = _rope(k_ref[r, :].astype(F32), cos_ref[r, :], sin_ref[r, :])
        k_scr[r, :] = k.astype(BF16)
        v = v_ref[r, :].astype(F32)
        vz = jnp.concatenate([v * zeta_f, v * zeta_b], axis=1).astype(BF16)
        kt = k.T.astype(BF16)
        kv_scr[c] = jnp.dot(kt, vz, preferred_element_type=F32)

    for c in range(n_chunks):
        chunk_kv(c)

    dec_f = jnp.exp(chunk * lgf_l)
    dec_b = jnp.exp(chunk * lgb_l)
    sf = jnp.zeros((HEAD_DIM, LANES), F32)
    sb = jnp.zeros((HEAD_DIM, LANES), F32)
    for c in range(n_chunks):
        cb = n_chunks - 1 - c
        st_scr[c, :, :LANES] = sf.astype(BF16)
        st_scr[cb, :, LANES:] = sb.astype(BF16)
        sf = dec_f * sf + kv_scr[c, :, :LANES]
        sb = dec_b * sb + kv_scr[cb, :, LANES:]

    gn = gn_ref[...]

    def chunk_out(c):
        r = rows(c)
        q = _rope(q_ref[r, :].astype(F32), cos_ref[r, :], sin_ref[r, :]).astype(BF16)
        a = lax.dot_general(q, k_scr[r, :], (((1,), (1,)), ((), ())), preferred_element_type=F32) * dmask
        y = jnp.dot(a.astype(BF16), v_ref[r, :].astype(BF16), preferred_element_type=F32)
        cr = jnp.dot(q, st_scr[c], preferred_element_type=F32)
        y = y + cr[:, :LANES] * inner_f + cr[:, LANES:] * inner_b
        mu = jnp.mean(y, axis=-1, keepdims=True)
        yc = y - mu
        var = jnp.mean(yc * yc, axis=-1, keepdims=True)
        yn = yc * lax.rsqrt(var + EPS)
        g = g_ref[r, :].astype(F32)
        silu = g * (1.0 / (1.0 + jnp.exp(-g)))
        o_ref[r, :] = (yn * gn * silu).astype(o_ref.dtype)

    for c in range(n_chunks):
        chunk_out(c)


def _retention(proj, cos, sin, dec, gn, batch, seq):
    m = proj.shape[0]
    n_chunks = seq // RET_CHUNK
    col = lambda off: (lambda b, h: (b, off // LANES + h))
    return pl.pallas_call(
        functools.partial(_ret_kernel, chunk=RET_CHUNK),
        grid=(batch, RET_HEADS),
        in_specs=[
            pl.BlockSpec((seq, LANES), col(OFF_RQ)),
            pl.BlockSpec((seq, LANES), col(OFF_RK)),
            pl.BlockSpec((seq, LANES), col(OFF_RV)),
            pl.BlockSpec((seq, LANES), col(OFF_RG)),
            pl.BlockSpec((seq, LANES), lambda b, h: (0, 0)),
            pl.BlockSpec((seq, LANES), lambda b, h: (0, 0)),
            pl.BlockSpec((1, 2, RET_CHUNK), lambda b, h: (h, 0, 0)),
            pl.BlockSpec((1, LANES), lambda b, h: (0, h)),
        ],
        out_specs=pl.BlockSpec((seq, LANES), lambda b, h: (b, h)),
        out_shape=jax.ShapeDtypeStruct((m, RET_WIDTH), BF16),
        scratch_shapes=[
            pltpu.VMEM((n_chunks, HEAD_DIM, 2 * LANES), F32),
            pltpu.VMEM((n_chunks, HEAD_DIM, 2 * LANES), BF16),
            pltpu.VMEM((seq, HEAD_DIM), BF16),
        ],
        compiler_params=_params("parallel", "parallel"),
        name="retention",
    )(proj, proj, proj, proj, cos, sin, dec, gn)


def _attn_kernel(q_ref, k_ref, v_ref, o_ref, *, tk, rb, d_qk, kv_heads):
    seq = k_ref.shape[0]
    nr = q_ref.shape[0] // rb
    q_heads = q_ref.shape[1] // d_qk
    for hd in range(q_heads):
        kv = hd * kv_heads // q_heads
        kc = slice(kv * d_qk, (kv + 1) * d_qk)
        vc = slice(kv * V_AUG, (kv + 1) * V_AUG)
        for r in range(nr):
            rows = slice(r * rb, (r + 1) * rb)
            q = q_ref[rows, hd * d_qk:(hd + 1) * d_qk]
            m = acc = None
            for j in range(seq // tk):
                ks = slice(j * tk, (j + 1) * tk)
                s = lax.dot_general(q, k_ref[ks, kc], (((1,), (1,)), ((), ())), preferred_element_type=F32)
                mj = jnp.max(s, axis=-1, keepdims=True)
                if j == 0:
                    m = mj
                    acc = jnp.dot(jnp.exp2(s - m).astype(BF16), v_ref[ks, vc], preferred_element_type=F32)
                else:
                    m_new = jnp.maximum(m, mj)
                    alpha = jnp.exp2(m - m_new)
                    acc = alpha * acc + jnp.dot(jnp.exp2(s - m_new).astype(BF16), v_ref[ks, vc],
                                                preferred_element_type=F32)
                    m = m_new
            o_ref[rows, hd * LANES:(hd + 1) * LANES] = (acc[:, :LANES] / acc[:, LANES:]).astype(o_ref.dtype)


def _attention(q, k, v, batch, seq, heads, group, d_qk):
    m = q.shape[0]
    tq = min(seq, ROWS_ATTN)
    nq = seq // tq
    hps = ROWS_ATTN // tq
    kv_heads = 1 if group >= hps else hps // group
    d_v = HEAD_DIM
    return pl.pallas_call(
        functools.partial(_attn_kernel, tk=TK_LONG_ATTN if tq == ROWS_ATTN else TK_SHORT_ATTN, rb=RB_ATTN,
                          d_qk=d_qk, kv_heads=kv_heads),
        grid=(batch, heads // hps, nq),
        in_specs=[
            pl.BlockSpec((tq, hps * d_qk), lambda b, h, i: (b * nq + i, h)),
            pl.BlockSpec((seq, kv_heads * d_qk), lambda b, h, i: (b, h * hps // (group * kv_heads))),
            pl.BlockSpec((seq, kv_heads * V_AUG), lambda b, h, i: (b, h * hps // (group * kv_heads))),
        ],
        out_specs=pl.BlockSpec((tq, hps * d_v), lambda b, h, i: (b * nq + i, h)),
        out_shape=jax.ShapeDtypeStruct((m, heads * d_v), BF16),
        compiler_params=_params("parallel", "parallel", "parallel"),
        name="attention",
    )(q, k, v)


def _outproj_kernel(x_ref, a_ref, b_ref, c_ref, w_ref, o_ref):
    wa = a_ref.shape[1]
    wb = b_ref.shape[1]
    acc = x_ref[...]
    acc = acc + jnp.dot(a_ref[...], w_ref[0:wa, :], preferred_element_type=F32)
    acc = acc + jnp.dot(b_ref[...], w_ref[wa:wa + wb, :], preferred_element_type=F32)
    acc = acc + jnp.dot(c_ref[...], w_ref[wa + wb:, :], preferred_element_type=F32)
    o_ref[...] = acc


def _outproj(x, a, b, c, w, l):
    m = x.shape[0]
    tm = TM_PROJ
    row = lambda width: pl.BlockSpec((tm, width), lambda i: (i, 0))
    return pl.pallas_call(
        _outproj_kernel,
        grid=(m // tm,),
        in_specs=[
            row(D_MODEL), row(a.shape[1]), row(b.shape[1]), row(c.shape[1]),
            pl.BlockSpec((None, D_MODEL, D_MODEL), lambda i: (l, 0, 0), pipeline_mode=pl.Buffered(1)),
        ],
        out_specs=row(D_MODEL),
        out_shape=jax.ShapeDtypeStruct((m, D_MODEL), F32),
        compiler_params=_params("parallel"),
        name="out_proj",
    )(x, a, b, c, w)


def _mlp_kernel(x_ref, g_ref, wu_ref, wd_ref, o_ref, h_scr):
    @pl.when(pl.program_id(1) == 0)
    def _():
        x = x_ref[...]
        h_scr[...] = _rms(x, g_ref[...]).astype(BF16)
        o_ref[...] = x

    u = jnp.dot(h_scr[...], wu_ref[...], preferred_element_type=F32)
    u = jnp.square(jnp.maximum(u, 0.0)).astype(BF16)
    o_ref[...] += jnp.dot(u, wd_ref[...], preferred_element_type=F32)


def _mlp(x, g, wu, wd, l):
    m = x.shape[0]
    tm = TM_MLP
    tf = TF_MLP
    return pl.pallas_call(
        _mlp_kernel,
        grid=(m // tm, D_FF // tf),
        in_specs=[
            pl.BlockSpec((tm, D_MODEL), lambda i, f: (i, 0)),
            pl.BlockSpec((1, D_MODEL), lambda i, f: (0, 0)),
            pl.BlockSpec((None, D_MODEL, tf), lambda i, f: (l, 0, f)),
            pl.BlockSpec((None, tf, D_MODEL), lambda i, f: (l, f, 0)),
        ],
        out_specs=pl.BlockSpec((tm, D_MODEL), lambda i, f: (i, 0)),
        out_shape=jax.ShapeDtypeStruct((m, D_MODEL), F32),
        scratch_shapes=[pltpu.VMEM((tm, D_MODEL), BF16)],
        compiler_params=_params("parallel", "arbitrary"),
        name="mlp",
    )(x, g, wu, wd)


def _rope_tables(seq):
    t = jnp.arange(seq, dtype=F32)

    def half_tables(ang):
        c, s = jnp.cos(ang), jnp.sin(ang)
        return jnp.concatenate([c, c], axis=-1), jnp.concatenate([-s, s], axis=-1)

    inv = 1.0 / (ROPE_THETA ** (jnp.arange(0, HEAD_DIM, 2, dtype=F32) / HEAD_DIM))
    ret = half_tables(t[:, None] * inv[None, :])

    ti = jnp.arange(seq)
    row = (ti // GRID_W).astype(F32)
    colp = (ti % GRID_W).astype(F32)

    def axial(dim):
        half = dim // 2
        iv = 1.0 / (ROPE_THETA ** (jnp.arange(0, half, 2, dtype=F32) / half))
        return jnp.concatenate([row[:, None] * iv[None, :], colp[:, None] * iv[None, :]], axis=-1)

    ax128 = half_tables(axial(HEAD_DIM))
    a64 = axial(MLA_ROPE)
    c, s = jnp.cos(a64), jnp.sin(a64)
    z = jnp.zeros_like(c)
    ax64 = (jnp.concatenate([c, z, c, z], axis=-1), jnp.concatenate([-s, z, s, z], axis=-1))
    return ret, ax128, ax64


def _pad_rope_cols(w):
    z = jnp.zeros(w.shape[:-1] + (MLA_ROPE // 2,), w.dtype)
    return jnp.concatenate([w[..., :MLA_ROPE // 2], z, w[..., MLA_ROPE // 2:], z], axis=-1)


def _matmul_weights(w_in, mla_w_uq, mla_w_ukv, w_out, w_up, w_down):
    depth = w_in.shape[0]
    kr0 = OFF_KROPE
    w_in_p = jnp.concatenate(
        [w_in[..., :kr0], _pad_rope_cols(w_in[..., kr0:kr0 + MLA_ROPE]), w_in[..., kr0 + MLA_ROPE:]], axis=-1)
    uq = mla_w_uq.reshape(depth, MLA_Q_RANK, MLA_HEADS, MLA_QK)
    uq_p = jnp.concatenate([uq[..., :MLA_NOPE], _pad_rope_cols(uq[..., MLA_NOPE:])], axis=-1)
    uq_p = uq_p.reshape(depth, MLA_Q_RANK, MLA_HEADS * MLA_PAD)
    return dict(w_in=w_in_p.astype(BF16), wuq=uq_p.astype(BF16), wukv=mla_w_ukv.astype(BF16),
                w_out=w_out.astype(BF16), w_up=w_up.astype(BF16), w_down=w_down.astype(BF16))


def _layer_params(l, ln1_w, ret_decay_fwd, ret_decay_bwd, ret_gn_w, mla_q_a_norm, mla_kv_a_norm, mla_q_norm,
                  mla_k_norm, gqa_q_norm, gqa_k_norm, ln2_w):
    qn = mla_q_norm[l]
    gq = (jnp.concatenate([qn[:MLA_NOPE], _pad_rope_cols(qn[MLA_NOPE:])]) * (MLA_QK ** -0.5 * LOG2E)).reshape(1, MLA_PAD)
    kn = mla_k_norm[l]
    lg = jnp.stack([jax.nn.log_sigmoid(ret_decay_fwd[l].astype(F32)), jax.nn.log_sigmoid(ret_decay_bwd[l].astype(F32))],
                   axis=1)
    return dict(
        l=l,
        ln1=ln1_w[l].reshape(1, D_MODEL),
        dec=jnp.broadcast_to(lg[:, :, None], (RET_HEADS, 2, RET_CHUNK)),
        gn=ret_gn_w[l].reshape(1, RET_WIDTH),
        gqa_n=mla_q_a_norm[l].reshape(1, MLA_Q_RANK),
        gkva_n=mla_kv_a_norm[l].reshape(1, MLA_KV_RANK),
        gq=gq,
        gkn=kn[:MLA_NOPE].reshape(1, LANES),
        gkr=_pad_rope_cols(kn[MLA_NOPE:]).reshape(1, LANES),
        nq=(gqa_q_norm[l] * (HEAD_DIM ** -0.5 * LOG2E)).reshape(1, LANES),
        nk=gqa_k_norm[l].reshape(1, LANES),
        ln2=ln2_w[l].reshape(1, D_MODEL),
    )


def _layer(x, w, p, tables, batch, seq):
    (ret_c, ret_s), (ax_c, ax_s), (m_c, m_s) = tables
    l = p["l"]
    rqkvg, qm, km, vm, qg, kg, vg = _inproj(x, w, p, m_c, m_s, ax_c, ax_s, seq)
    ret = _retention(rqkvg, ret_c, ret_s, p["dec"], p["gn"], batch, seq)
    mla = _attention(qm, km, vm, batch, seq, MLA_HEADS, 1, MLA_PAD)
    gqa = _attention(qg, kg, vg, batch, seq, GQA_HEADS, GQA_GROUP, HEAD_DIM)
    x = _outproj(x, ret, mla, gqa, w["w_out"], l)
    return _mlp(x, p["ln2"], w["w_up"], w["w_down"], l)


def kernel(x_prompt, x_sample, ln1_w, w_in, ret_decay_fwd, ret_decay_bwd, ret_gn_w, mla_q_a_norm, mla_w_uq,
           mla_kv_a_norm, mla_w_ukv, mla_q_norm, mla_k_norm, gqa_q_norm, gqa_k_norm, w_out, ln2_w, w_up, w_down):
    depth = w_in.shape[0]
    w = _matmul_weights(w_in, mla_w_uq, mla_w_ukv, w_out, w_up, w_down)
    layers = [
        _layer_params(l, ln1_w, ret_decay_fwd, ret_decay_bwd, ret_gn_w, mla_q_a_norm, mla_kv_a_norm, mla_q_norm,
                      mla_k_norm, gqa_q_norm, gqa_k_norm, ln2_w)
        for l in range(depth)
    ]

    def trunk(x):
        batch, seq, _ = x.shape
        tables = _rope_tables(seq)
        y = x.reshape(batch * seq, D_MODEL)
        for p in layers:
            y = _layer(y, w, p, tables, batch, seq)
        return y.reshape(batch, seq, D_MODEL)

    return trunk(x_prompt), trunk(x_sample)
```

```python
import functools

import jax
import jax.numpy as jnp
from jax import lax
from jax.experimental import pallas as pl
from jax.experimental.pallas import tpu as pltpu

D_MODEL = 2048
HEAD_DIM = 128
RET_HEADS = 4
RET_WIDTH = RET_HEADS * HEAD_DIM
MLA_HEADS = 4
MLA_Q_RANK = 384
MLA_KV_RANK = 256
MLA_NOPE = 128
MLA_ROPE = 64
MLA_QK = MLA_NOPE + MLA_ROPE
MLA_V = 128
GQA_HEADS = 8
GQA_KV_HEADS = 2
GQA_GROUP = GQA_HEADS // GQA_KV_HEADS
D_FF = 4 * D_MODEL
GRID_W = 64
ROPE_THETA = 10000.0
EPS = 1e-6
LOG2E = 1.4426950408889634

LANES = 128
MLA_PAD = 2 * LANES

OFF_RQ = 0
OFF_RK = 512
OFF_RV = 1024
OFF_RG = 1536
OFF_CQ = 2048
OFF_CKV = 2432
OFF_KROPE = 2688
OFF_GQ = 2816
OFF_GK = 3840
OFF_GV = 4096
IN_PAD = 4352
V_AUG = 2 * LANES

TM_PROJ = 512
TM_MLP = 512
TF_MLP = 1024
ROWS_ATTN = 4096
RB_ATTN = 512
TK_LONG_ATTN = 1024
TK_SHORT_ATTN = 256
RET_CHUNK = 256
VMEM_LIMIT = 52 * 1024 * 1024

F32 = jnp.float32
BF16 = jnp.bfloat16


def _params(*sem):
    return pltpu.CompilerParams(dimension_semantics=sem, vmem_limit_bytes=VMEM_LIMIT)


def _rms(x, g):
    ms = jnp.mean(x * x, axis=-1, keepdims=True)
    return x * lax.rsqrt(ms + EPS) * g


def _rope(x, c, s):
    return x * c + pltpu.roll(x, LANES // 2, 1) * s


def _inproj_kernel(x_ref, g_ref, w_ref, gqa_ref, gkva_ref, wuq_ref, wukv_ref, gq_ref, gkn_ref, gkr_ref, nq_ref, nk_ref,
                   mcos_ref, msin_ref, acos_ref, asin_ref, ret_out, qm_out, km_out, vm_out, qg_out, kg_out, vg_out):
    h = _rms(x_ref[...], g_ref[...]).astype(BF16)
    ones = jnp.ones((x_ref.shape[0], LANES), BF16)

    def proj(c0, c1):
        return jnp.dot(h, w_ref[:, c0:c1], preferred_element_type=F32)

    split = (OFF_CQ + OFF_GV) // 2
    pa = proj(OFF_CQ, split)
    pb = proj(split, OFF_GV)

    def cols(off, width):
        if off < split:
            return pa[:, off - OFF_CQ:off - OFF_CQ + width]
        return pb[:, off - split:off - split + width]

    cqn = _rms(cols(OFF_CQ, MLA_Q_RANK), gqa_ref[...]).astype(BF16)
    qm = jnp.dot(cqn, wuq_ref[...], preferred_element_type=F32)
    ckvn = _rms(cols(OFF_CKV, MLA_KV_RANK), gkva_ref[...]).astype(BF16)
    kvm = jnp.dot(ckvn, wukv_ref[...], preferred_element_type=F32)
    cs = mcos_ref[...]
    sn = msin_ref[...]
    kr = cols(OFF_KROPE, LANES)
    kr_ss = jnp.sum(kr * kr, axis=-1, keepdims=True)
    kr_rot = _rope(kr * gkr_ref[...], cs, sn)
    gq_n = gq_ref[:, :LANES]
    gq_r = gq_ref[:, LANES:]
    gk_n = gkn_ref[...]
    inv_d = 1.0 / MLA_QK
    for hd in range(MLA_HEADS):
        o = MLA_PAD * hd
        qn = qm[:, o:o + LANES]
        qr = qm[:, o + LANES:o + MLA_PAD]
        rq = lax.rsqrt((jnp.sum(qn * qn, axis=-1, keepdims=True) + jnp.sum(qr * qr, axis=-1, keepdims=True)) * inv_d
                       + EPS)
        qm_out[:, o:o + LANES] = (qn * rq * gq_n).astype(BF16)
        qm_out[:, o + LANES:o + MLA_PAD] = _rope(qr * rq * gq_r, cs, sn).astype(BF16)
        kn = kvm[:, o:o + LANES]
        rk = lax.rsqrt((jnp.sum(kn * kn, axis=-1, keepdims=True) + kr_ss) * inv_d + EPS)
        km_out[:, o:o + LANES] = (kn * rk * gk_n).astype(BF16)
        km_out[:, o + LANES:o + MLA_PAD] = (kr_rot * rk).astype(BF16)
        vm_out[:, o:o + LANES] = kvm[:, o + LANES:o + MLA_PAD].astype(BF16)
        vm_out[:, o + LANES:o + V_AUG] = ones

    cs = acos_ref[...]
    sn = asin_ref[...]
    nq = nq_ref[...]
    nk = nk_ref[...]
    for hd in range(GQA_HEADS):
        qg_out[:, LANES * hd:LANES * (hd + 1)] = _rope(_rms(cols(OFF_GQ + LANES * hd, LANES), nq), cs, sn).astype(BF16)
    for hd in range(GQA_KV_HEADS):
        kg_out[:, LANES * hd:LANES * (hd + 1)] = _rope(_rms(cols(OFF_GK + LANES * hd, LANES), nk), cs, sn).astype(BF16)

    gv = proj(OFF_GV, IN_PAD).astype(BF16)
    for hd in range(GQA_KV_HEADS):
        vg_out[:, V_AUG * hd:V_AUG * hd + LANES] = gv[:, LANES * hd:LANES * (hd + 1)]
        vg_out[:, V_AUG * hd + LANES:V_AUG * (hd + 1)] = ones

    for c0 in range(0, OFF_CQ, 1024):
        ret_out[:, c0:c0 + 1024] = proj(c0, c0 + 1024).astype(BF16)


def _inproj(x, w, p, mcos, msin, acos, asin, seq):
    m = x.shape[0]
    tm = TM_PROJ
    nseq = seq // tm
    l = p["l"]
    full = lambda shape: pl.BlockSpec(shape, lambda i: (0, 0))
    layer = lambda shape: pl.BlockSpec((None,) + shape, lambda i: (l, 0, 0))
    table = pl.BlockSpec((tm, LANES), lambda i: (i % nseq, 0))
    widths = (OFF_CQ, MLA_HEADS * MLA_PAD, MLA_HEADS * MLA_PAD, MLA_HEADS * V_AUG, GQA_HEADS * HEAD_DIM,
              GQA_KV_HEADS * HEAD_DIM, GQA_KV_HEADS * V_AUG)
    return pl.pallas_call(
        _inproj_kernel,
        grid=(m // tm,),
        in_specs=[
            pl.BlockSpec((tm, D_MODEL), lambda i: (i, 0)),
            full((1, D_MODEL)),
            pl.BlockSpec((None, D_MODEL, IN_PAD), lambda i: (l, 0, 0), pipeline_mode=pl.Buffered(1)),
            full((1, MLA_Q_RANK)),
            full((1, MLA_KV_RANK)),
            layer((MLA_Q_RANK, MLA_HEADS * MLA_PAD)),
            layer((MLA_KV_RANK, MLA_HEADS * MLA_PAD)),
            full((1, MLA_PAD)),
            full((1, LANES)),
            full((1, LANES)),
            full((1, LANES)),
            full((1, LANES)),
            table, table, table, table,
        ],
        out_specs=[pl.BlockSpec((tm, wd), lambda i: (i, 0)) for wd in widths],
        out_shape=[jax.ShapeDtypeStruct((m, wd), BF16) for wd in widths],
        compiler_params=_params("parallel"),
        name="in_proj",
    )(x, p["ln1"], w["w_in"], p["gqa_n"], p["gkva_n"], w["wuq"], w["wukv"], p["gq"], p["gkn"], p["gkr"], p["nq"],
      p["nk"], mcos, msin, acos, asin)


def _ret_kernel(q_ref, k_ref, v_ref, g_ref, cos_ref, sin_ref, dec_ref, gn_ref, o_ref, kv_scr, st_scr, k_scr, *, chunk):
    seq = q_ref.shape[0]
    n_chunks = seq // chunk
    lgf = dec_ref[0, 0:1, :]
    lgb = dec_ref[0, 1:2, :]
    lgf_l = lgf[:, :LANES]
    lgb_l = lgb[:, :LANES]
    k_scale = HEAD_DIM ** -0.5
    pos = lax.broadcasted_iota(jnp.int32, (chunk, LANES), 0).astype(F32)
    zeta_f = jnp.exp((chunk - 1.0 - pos) * lgf_l)
    zeta_b = jnp.exp(pos * lgb_l)
    inner_f = jnp.exp((pos + 1.0) * lgf_l) * k_scale
    inner_b = jnp.exp((chunk - pos) * lgb_l) * k_scale
    ii = lax.broadcasted_iota(jnp.int32, (chunk, chunk), 0)
    jj = lax.broadcasted_iota(jnp.int32, (chunk, chunk), 1)
    d = (ii - jj).astype(F32)
    dmask = jnp.where(d >= 0, jnp.exp(jnp.maximum(d, 0.0) * lgf), jnp.exp(jnp.maximum(-d, 0.0) * lgb)) * k_scale

    def rows(c):
        return slice(c * chunk, (c + 1) * chunk)

    def chunk_kv(c):
        r = rows(c)
        k = _rope(k_ref[r, :].astype(F32), cos_ref[r, :], sin_ref[r, :])
        k_scr[r, :] = k.astype(BF16)
        v = v_ref[r, :].astype(F32)
        vz = jnp.concatenate([v * zeta_f, v * zeta_b], axis=1).astype(BF16)
        kt = k.T.astype(BF16)
        kv_scr[c] = jnp.dot(kt, vz, preferred_element_type=F32)

    for c in range(n_chunks):
        chunk_kv(c)

    dec_f = jnp.exp(chunk * lgf_l)
    dec_b = jnp.exp(chunk * lgb_l)
    sf = jnp.zeros((HEAD_DIM, LANES), F32)
    sb = jnp.zeros((HEAD_DIM, LANES), F32)
    for c in range(n_chunks):
        cb = n_chunks - 1 - c
        st_scr[c, :, :LANES] = sf.astype(BF16)
        st_scr[cb, :, LANES:] = sb.astype(BF16)
        sf = dec_f * sf + kv_scr[c, :, :LANES]
        sb = dec_b * sb + kv_scr[cb, :, LANES:]

    gn = gn_ref[...]

    def chunk_out(c):
        r = rows(c)
        q = _rope(q_ref[r, :].astype(F32), cos_ref[r, :], sin_ref[r, :]).astype(BF16)
        a = lax.dot_general(q, k_scr[r, :], (((1,), (1,)), ((), ())), preferred_element_type=F32) * dmask
        y = jnp.dot(a.astype(BF16), v_ref[r, :].astype(BF16), preferred_element_type=F32)
        cr = jnp.dot(q, st_scr[c], preferred_element_type=F32)
        y = y + cr[:, :LANES] * inner_f + cr[:, LANES:] * inner_b
        mu = jnp.mean(y, axis=-1, keepdims=True)
        yc = y - mu
        var = jnp.mean(yc * yc, axis=-1, keepdims=True)
        yn = yc * lax.rsqrt(var + EPS)
        g = g_ref[r, :].astype(F32)
        silu = g * (1.0 / (1.0 + jnp.exp(-g)))
        o_ref[r, :] = (yn * gn * silu).astype(o_ref.dtype)

    for c in range(n_chunks):
        chunk_out(c)


def _retention(proj, cos, sin, dec, gn, batch, seq):
    m = proj.shape[0]
    n_chunks = seq // RET_CHUNK
    col = lambda off: (lambda b, h: (b, off // LANES + h))
    return pl.pallas_call(
        functools.partial(_ret_kernel, chunk=RET_CHUNK),
        grid=(batch, RET_HEADS),
        in_specs=[
            pl.BlockSpec((seq, LANES), col(OFF_RQ)),
            pl.BlockSpec((seq, LANES), col(OFF_RK)),
            pl.BlockSpec((seq, LANES), col(OFF_RV)),
            pl.BlockSpec((seq, LANES), col(OFF_RG)),
            pl.BlockSpec((seq, LANES), lambda b, h: (0, 0)),
            pl.BlockSpec((seq, LANES), lambda b, h: (0, 0)),
            pl.BlockSpec((1, 2, RET_CHUNK), lambda b, h: (h, 0, 0)),
            pl.BlockSpec((1, LANES), lambda b, h: (0, h)),
        ],
        out_specs=pl.BlockSpec((seq, LANES), lambda b, h: (b, h)),
        out_shape=jax.ShapeDtypeStruct((m, RET_WIDTH), BF16),
        scratch_shapes=[
            pltpu.VMEM((n_chunks, HEAD_DIM, 2 * LANES), F32),
            pltpu.VMEM((n_chunks, HEAD_DIM, 2 * LANES), BF16),
            pltpu.VMEM((seq, HEAD_DIM), BF16),
        ],
        compiler_params=_params("parallel", "parallel"),
        name="retention",
    )(proj, proj, proj, proj, cos, sin, dec, gn)


def _attn_kernel(q_ref, k_ref, v_ref, o_ref, *, tk, rb, d_qk, kv_heads):
    seq = k_ref.shape[0]
    nr = q_ref.shape[0] // rb
    q_heads = q_ref.shape[1] // d_qk
    for hd in range(q_heads):
        kv = hd * kv_heads // q_heads
        kc = slice(kv * d_qk, (kv + 1) * d_qk)
        vc = slice(kv * V_AUG, (kv + 1) * V_AUG)
        for r in range(nr):
            rows = slice(r * rb, (r + 1) * rb)
            q = q_ref[rows, hd * d_qk:(hd + 1) * d_qk]
            m = acc = None
            for j in range(seq // tk):
                ks = slice(j * tk, (j + 1) * tk)
                s = lax.dot_general(q, k_ref[ks, kc], (((1,), (1,)), ((), ())), preferred_element_type=F32)
                mj = jnp.max(s, axis=-1, keepdims=True)
                if j == 0:
                    m = mj
                    acc = jnp.dot(jnp.exp2(s - m).astype(BF16), v_ref[ks, vc], preferred_element_type=F32)
                else:
                    m_new = jnp.maximum(m, mj)
                    alpha = jnp.exp2(m - m_new)
                    acc = alpha * acc + jnp.dot(jnp.exp2(s - m_new).astype(BF16), v_ref[ks, vc],
                                                preferred_element_type=F32)
                    m = m_new
            o_ref[rows, hd * LANES:(hd + 1) * LANES] = (acc[:, :LANES] / acc[:, LANES:]).astype(o_ref.dtype)


def _attention(q, k, v, batch, seq, heads, group, d_qk):
    m = q.shape[0]
    tq = min(seq, ROWS_ATTN)
    nq = seq // tq
    hps = ROWS_ATTN // tq
    kv_heads = 1 if group >= hps else hps // group
    d_v = HEAD_DIM
    return pl.pallas_call(
        functools.partial(_attn_kernel, tk=TK_LONG_ATTN if tq == ROWS_ATTN else TK_SHORT_ATTN, rb=RB_ATTN,
                          d_qk=d_qk, kv_heads=kv_heads),
        grid=(batch, heads // hps, nq),
        in_specs=[
            pl.BlockSpec((tq, hps * d_qk), lambda b, h, i: (b * nq + i, h)),
            pl.BlockSpec((seq, kv_heads * d_qk), lambda b, h, i: (b, h * hps // (group * kv_heads))),
            pl.BlockSpec((seq, kv_heads * V_AUG), lambda b, h, i: (b, h * hps // (group * kv_heads))),
        ],
        out_specs=pl.BlockSpec((tq, hps * d_v), lambda b, h, i: (b * nq + i, h)),
        out_shape=jax.ShapeDtypeStruct((m, heads * d_v), BF16),
        compiler_params=_params("parallel", "parallel", "parallel"),
        name="attention",
    )(q, k, v)


def _outproj_kernel(x_ref, a_ref, b_ref, c_ref, w_ref, wsrc_ref, o_ref, wdst_ref):
    wa = a_ref.shape[1]
    wb = b_ref.shape[1]
    acc = x_ref[...]
    acc = acc + jnp.dot(a_ref[...], w_ref[0:wa, :], preferred_element_type=F32)
    acc = acc + jnp.dot(b_ref[...], w_ref[wa:wa + wb, :], preferred_element_type=F32)
    acc = acc + jnp.dot(c_ref[...], w_ref[wa + wb:, :], preferred_element_type=F32)
    o_ref[...] = acc
    wdst_ref[...] = wsrc_ref[...].astype(BF16)


def _outproj(x, a, b, c, w, l, w_mlp):
    m = x.shape[0]
    tm = TM_PROJ
    steps = m // tm
    _, w_rows, w_cols = w_mlp.shape
    assert w_rows % steps == 0
    slab = w_rows // steps
    row = lambda width: pl.BlockSpec((tm, width), lambda i: (i, 0))
    return pl.pallas_call(
        _outproj_kernel,
        grid=(steps,),
        in_specs=[
            row(D_MODEL), row(a.shape[1]), row(b.shape[1]), row(c.shape[1]),
            pl.BlockSpec((None, D_MODEL, D_MODEL), lambda i: (l, 0, 0), pipeline_mode=pl.Buffered(1)),
            pl.BlockSpec((None, slab, w_cols), lambda i: (l, i, 0)),
        ],
        out_specs=[row(D_MODEL), pl.BlockSpec((slab, w_cols), lambda i: (i, 0))],
        out_shape=[jax.ShapeDtypeStruct((m, D_MODEL), F32), jax.ShapeDtypeStruct((w_rows, w_cols), BF16)],
        compiler_params=_params("parallel"),
        name="out_proj",
    )(x, a, b, c, w, w_mlp)


def _mlp_kernel(x_ref, g_ref, wu_ref, wd_ref, o_ref, h_scr):
    @pl.when(pl.program_id(1) == 0)
    def _():
        x = x_ref[...]
        h_scr[...] = _rms(x, g_ref[...]).astype(BF16)
        o_ref[...] = x

    u = jnp.dot(h_scr[...], wu_ref[...], preferred_element_type=F32)
    u = jnp.square(jnp.maximum(u, 0.0)).astype(BF16)
    o_ref[...] += jnp.dot(u, wd_ref[...], preferred_element_type=F32)


def _mlp(x, g, wu, wd):
    m = x.shape[0]
    tm = TM_MLP
    tf = TF_MLP
    return pl.pallas_call(
        _mlp_kernel,
        grid=(m // tm, D_FF // tf),
        in_specs=[
            pl.BlockSpec((tm, D_MODEL), lambda i, f: (i, 0)),
            pl.BlockSpec((1, D_MODEL), lambda i, f: (0, 0)),
            pl.BlockSpec((D_MODEL, tf), lambda i, f: (0, f)),
            pl.BlockSpec((tf, D_MODEL), lambda i, f: (f, 0)),
        ],
        out_specs=pl.BlockSpec((tm, D_MODEL), lambda i, f: (i, 0)),
        out_shape=jax.ShapeDtypeStruct((m, D_MODEL), F32),
        scratch_shapes=[pltpu.VMEM((tm, D_MODEL), BF16)],
        compiler_params=_params("parallel", "arbitrary"),
        name="mlp",
    )(x, g, wu, wd)


def _rope_tables(seq):
    t = jnp.arange(seq, dtype=F32)

    def half_tables(ang):
        c, s = jnp.cos(ang), jnp.sin(ang)
        return jnp.concatenate([c, c], axis=-1), jnp.concatenate([-s, s], axis=-1)

    inv = 1.0 / (ROPE_THETA ** (jnp.arange(0, HEAD_DIM, 2, dtype=F32) / HEAD_DIM))
    ret = half_tables(t[:, None] * inv[None, :])

    ti = jnp.arange(seq)
    row = (ti // GRID_W).astype(F32)
    colp = (ti % GRID_W).astype(F32)

    def axial(dim):
        half = dim // 2
        iv = 1.0 / (ROPE_THETA ** (jnp.arange(0, half, 2, dtype=F32) / half))
        return jnp.concatenate([row[:, None] * iv[None, :], colp[:, None] * iv[None, :]], axis=-1)

    ax128 = half_tables(axial(HEAD_DIM))
    a64 = axial(MLA_ROPE)
    c, s = jnp.cos(a64), jnp.sin(a64)
    z = jnp.zeros_like(c)
    ax64 = (jnp.concatenate([c, z, c, z], axis=-1), jnp.concatenate([-s, z, s, z], axis=-1))
    return ret, ax128, ax64


def _pad_rope_cols(w):
    z = jnp.zeros(w.shape[:-1] + (MLA_ROPE // 2,), w.dtype)
    return jnp.concatenate([w[..., :MLA_ROPE // 2], z, w[..., MLA_ROPE // 2:], z], axis=-1)


def _matmul_weights(w_in, mla_w_uq, mla_w_ukv, w_out):
    depth = w_in.shape[0]
    kr0 = OFF_KROPE
    w_in_p = jnp.concatenate(
        [w_in[..., :kr0], _pad_rope_cols(w_in[..., kr0:kr0 + MLA_ROPE]), w_in[..., kr0 + MLA_ROPE:]], axis=-1)
    uq = mla_w_uq.reshape(depth, MLA_Q_RANK, MLA_HEADS, MLA_QK)
    uq_p = jnp.concatenate([uq[..., :MLA_NOPE], _pad_rope_cols(uq[..., MLA_NOPE:])], axis=-1)
    uq_p = uq_p.reshape(depth, MLA_Q_RANK, MLA_HEADS * MLA_PAD)
    return dict(w_in=w_in_p.astype(BF16), wuq=uq_p.astype(BF16), wukv=mla_w_ukv.astype(BF16),
                w_out=w_out.astype(BF16))


def _layer_params(l, ln1_w, ret_decay_fwd, ret_decay_bwd, ret_gn_w, mla_q_a_norm, mla_kv_a_norm, mla_q_norm,
                  mla_k_norm, gqa_q_norm, gqa_k_norm, ln2_w):
    qn = mla_q_norm[l]
    gq = (jnp.concatenate([qn[:MLA_NOPE], _pad_rope_cols(qn[MLA_NOPE:])]) * (MLA_QK ** -0.5 * LOG2E)).reshape(1, MLA_PAD)
    kn = mla_k_norm[l]
    lg = jnp.stack([jax.nn.log_sigmoid(ret_decay_fwd[l].astype(F32)), jax.nn.log_sigmoid(ret_decay_bwd[l].astype(F32))],
                   axis=1)
    return dict(
        l=l,
        ln1=ln1_w[l].reshape(1, D_MODEL),
        dec=jnp.broadcast_to(lg[:, :, None], (RET_HEADS, 2, RET_CHUNK)),
        gn=ret_gn_w[l].reshape(1, RET_WIDTH),
        gqa_n=mla_q_a_norm[l].reshape(1, MLA_Q_RANK),
        gkva_n=mla_kv_a_norm[l].reshape(1, MLA_KV_RANK),
        gq=gq,
        gkn=kn[:MLA_NOPE].reshape(1, LANES),
        gkr=_pad_rope_cols(kn[MLA_NOPE:]).reshape(1, LANES),
        nq=(gqa_q_norm[l] * (HEAD_DIM ** -0.5 * LOG2E)).reshape(1, LANES),
        nk=gqa_k_norm[l].reshape(1, LANES),
        ln2=ln2_w[l].reshape(1, D_MODEL),
    )


def _mixer(x, w, p, tables, batch, seq, w_mlp):
    (ret_c, ret_s), (ax_c, ax_s), (m_c, m_s) = tables
    rqkvg, qm, km, vm, qg, kg, vg = _inproj(x, w, p, m_c, m_s, ax_c, ax_s, seq)
    ret = _retention(rqkvg, ret_c, ret_s, p["dec"], p["gn"], batch, seq)
    mla = _attention(qm, km, vm, batch, seq, MLA_HEADS, 1, MLA_PAD)
    gqa = _attention(qg, kg, vg, batch, seq, GQA_HEADS, GQA_GROUP, HEAD_DIM)
    return _outproj(x, ret, mla, gqa, w["w_out"], p["l"], w_mlp)


def kernel(x_prompt, x_sample, ln1_w, w_in, ret_decay_fwd, ret_decay_bwd, ret_gn_w, mla_q_a_norm, mla_w_uq,
           mla_kv_a_norm, mla_w_ukv, mla_q_norm, mla_k_norm, gqa_q_norm, gqa_k_norm, w_out, ln2_w, w_up, w_down):
    depth = w_in.shape[0]
    w = _matmul_weights(w_in, mla_w_uq, mla_w_ukv, w_out)
    groups = [x_prompt, x_sample]
    mlp_weights = [w_up, w_down]
    tables = [_rope_tables(x.shape[1]) for x in groups]
    ys = [x.reshape(x.shape[0] * x.shape[1], D_MODEL) for x in groups]
    for l in range(depth):
        p = _layer_params(l, ln1_w, ret_decay_fwd, ret_decay_bwd, ret_gn_w, mla_q_a_norm, mla_kv_a_norm, mla_q_norm,
                          mla_k_norm, gqa_q_norm, gqa_k_norm, ln2_w)
        mixed = [_mixer(y, w, p, tables[g], groups[g].shape[0], groups[g].shape[1], mlp_weights[g])
                 for g, y in enumerate(ys)]
        (_, wu), (_, wd) = mixed
        ys = [_mlp(x1, p["ln2"], wu, wd) for x1, _ in mixed]
    return tuple(y.reshape(x.shape) for y, x in zip(ys, groups))
```

```python
import functools

import jax
import jax.numpy as jnp
from jax import lax
from jax.experimental import pallas as pl
from jax.experimental.pallas import tpu as pltpu

D_MODEL = 2048
HEAD_DIM = 128
RET_HEADS = 4
RET_WIDTH = RET_HEADS * HEAD_DIM
MLA_HEADS = 4
MLA_Q_RANK = 384
MLA_KV_RANK = 256
MLA_NOPE = 128
MLA_ROPE = 64
MLA_QK = MLA_NOPE + MLA_ROPE
MLA_V = 128
GQA_HEADS = 8
GQA_KV_HEADS = 2
GQA_GROUP = GQA_HEADS // GQA_KV_HEADS
D_FF = 4 * D_MODEL
GRID_W = 64
ROPE_THETA = 10000.0
EPS = 1e-6
LOG2E = 1.4426950408889634

LANES = 128
MLA_PAD = 2 * LANES

OFF_RQ = 0
OFF_RK = 512
OFF_RV = 1024
OFF_RG = 1536
OFF_CQ = 2048
OFF_CKV = 2432
OFF_KROPE = 2688
OFF_GQ = 2816
OFF_GK = 3840
OFF_GV = 4096
IN_PAD = 4352
V_AUG = 2 * LANES

TM_PROJ = 512
TM_MLP = 512
TF_MLP = 1024
ROWS_ATTN = 4096
RB_ATTN = 512
TK_LONG_ATTN = 1024
TK_SHORT_ATTN = 256
RET_CHUNK = 256
VMEM_LIMIT = 52 * 1024 * 1024

F32 = jnp.float32
BF16 = jnp.bfloat16


def _params(*sem):
    return pltpu.CompilerParams(dimension_semantics=sem, vmem_limit_bytes=VMEM_LIMIT)


def _rms(x, g):
    ms = jnp.mean(x * x, axis=-1, keepdims=True)
    return x * lax.rsqrt(ms + EPS) * g


def _rope(x, c, s):
    return x * c + pltpu.roll(x, LANES // 2, 1) * s


def _inproj_kernel(x_ref, g_ref, w_ref, gqa_ref, gkva_ref, wuq_ref, wukv_ref, gq_ref, gkn_ref, gkr_ref, nq_ref, nk_ref,
                   mcos_ref, msin_ref, acos_ref, asin_ref, ret_out, qm_out, km_out, vm_out, qg_out, kg_out, vg_out):
    h = _rms(x_ref[...], g_ref[...]).astype(BF16)
    ones = jnp.ones((x_ref.shape[0], LANES), BF16)

    def proj(c0, c1):
        return jnp.dot(h, w_ref[:, c0:c1], preferred_element_type=F32)

    split = (OFF_CQ + OFF_GV) // 2
    pa = proj(OFF_CQ, split)
    pb = proj(split, OFF_GV)

    def cols(off, width):
        if off < split:
            return pa[:, off - OFF_CQ:off - OFF_CQ + width]
        return pb[:, off - split:off - split + width]

    cqn = _rms(cols(OFF_CQ, MLA_Q_RANK), gqa_ref[...]).astype(BF16)
    qm = jnp.dot(cqn, wuq_ref[...], preferred_element_type=F32)
    ckvn = _rms(cols(OFF_CKV, MLA_KV_RANK), gkva_ref[...]).astype(BF16)
    kvm = jnp.dot(ckvn, wukv_ref[...], preferred_element_type=F32)
    cs = mcos_ref[...]
    sn = msin_ref[...]
    kr = cols(OFF_KROPE, LANES)
    kr_ss = jnp.sum(kr * kr, axis=-1, keepdims=True)
    kr_rot = _rope(kr * gkr_ref[...], cs, sn)
    gq_n = gq_ref[:, :LANES]
    gq_r = gq_ref[:, LANES:]
    gk_n = gkn_ref[...]
    inv_d = 1.0 / MLA_QK
    for hd in range(MLA_HEADS):
        o = MLA_PAD * hd
        qn = qm[:, o:o + LANES]
        qr = qm[:, o + LANES:o + MLA_PAD]
        rq = lax.rsqrt((jnp.sum(qn * qn, axis=-1, keepdims=True) + jnp.sum(qr * qr, axis=-1, keepdims=True)) * inv_d
                       + EPS)
        qm_out[:, o:o + LANES] = (qn * rq * gq_n).astype(BF16)
        qm_out[:, o + LANES:o + MLA_PAD] = _rope(qr * rq * gq_r, cs, sn).astype(BF16)
        kn = kvm[:, o:o + LANES]
        rk = lax.rsqrt((jnp.sum(kn * kn, axis=-1, keepdims=True) + kr_ss) * inv_d + EPS)
        km_out[:, o:o + LANES] = (kn * rk * gk_n).astype(BF16)
        km_out[:, o + LANES:o + MLA_PAD] = (kr_rot * rk).astype(BF16)
        vm_out[:, o:o + LANES] = kvm[:, o + LANES:o + MLA_PAD].astype(BF16)
        vm_out[:, o + LANES:o + V_AUG] = ones

    cs = acos_ref[...]
    sn = asin_ref[...]
    nq = nq_ref[...]
    nk = nk_ref[...]
    for hd in range(GQA_HEADS):
        qg_out[:, LANES * hd:LANES * (hd + 1)] = _rope(_rms(cols(OFF_GQ + LANES * hd, LANES), nq), cs, sn).astype(BF16)
    for hd in range(GQA_KV_HEADS):
        kg_out[:, LANES * hd:LANES * (hd + 1)] = _rope(_rms(cols(OFF_GK + LANES * hd, LANES), nk), cs, sn).astype(BF16)

    gv = proj(OFF_GV, IN_PAD).astype(BF16)
    for hd in range(GQA_KV_HEADS):
        vg_out[:, V_AUG * hd:V_AUG * hd + LANES] = gv[:, LANES * hd:LANES * (hd + 1)]
        vg_out[:, V_AUG * hd + LANES:V_AUG * (hd + 1)] = ones

    for c0 in range(0, OFF_CQ, 1024):
        ret_out[:, c0:c0 + 1024] = proj(c0, c0 + 1024).astype(BF16)


def _inproj(x, w, p, mcos, msin, acos, asin, seq):
    m = x.shape[0]
    tm = TM_PROJ
    nseq = seq // tm
    l = p["l"]
    full = lambda shape: pl.BlockSpec(shape, lambda i: (0, 0))
    layer = lambda shape: pl.BlockSpec((None,) + shape, lambda i: (l, 0, 0))
    table = pl.BlockSpec((tm, LANES), lambda i: (i % nseq, 0))
    widths = (OFF_CQ, MLA_HEADS * MLA_PAD, MLA_HEADS * MLA_PAD, MLA_HEADS * V_AUG, GQA_HEADS * HEAD_DIM,
              GQA_KV_HEADS * HEAD_DIM, GQA_KV_HEADS * V_AUG)
    return pl.pallas_call(
        _inproj_kernel,
        grid=(m // tm,),
        in_specs=[
            pl.BlockSpec((tm, D_MODEL), lambda i: (i, 0)),
            full((1, D_MODEL)),
            pl.BlockSpec((None, D_MODEL, IN_PAD), lambda i: (l, 0, 0), pipeline_mode=pl.Buffered(1)),
            full((1, MLA_Q_RANK)),
            full((1, MLA_KV_RANK)),
            layer((MLA_Q_RANK, MLA_HEADS * MLA_PAD)),
            layer((MLA_KV_RANK, MLA_HEADS * MLA_PAD)),
            full((1, MLA_PAD)),
            full((1, LANES)),
            full((1, LANES)),
            full((1, LANES)),
            full((1, LANES)),
            table, table, table, table,
        ],
        out_specs=[pl.BlockSpec((tm, wd), lambda i: (i, 0)) for wd in widths],
        out_shape=[jax.ShapeDtypeStruct((m, wd), BF16) for wd in widths],
        compiler_params=_params("parallel"),
        name="in_proj",
    )(x, p["ln1"], w["w_in"], p["gqa_n"], p["gkva_n"], w["wuq"], w["wukv"], p["gq"], p["gkn"], p["gkr"], p["nq"],
      p["nk"], mcos, msin, acos, asin)


def _ret_kernel(q_ref, k_ref, v_ref, g_ref, cos_ref, sin_ref, dec_ref, gn_ref, o_ref, kv_scr, st_scr, k_scr, *, chunk):
    seq = q_ref.shape[0]
    n_chunks = seq // chunk
    lgf = dec_ref[0, 0:1, :]
    lgb = dec_ref[0, 1:2, :]
    lgf_l = lgf[:, :LANES]
    lgb_l = lgb[:, :LANES]
    k_scale = HEAD_DIM ** -0.5
    pos = lax.broadcasted_iota(jnp.int32, (chunk, LANES), 0).astype(F32)
    zeta_f = jnp.exp((chunk - 1.0 - pos) * lgf_l)
    zeta_b = jnp.exp(pos * lgb_l)
    inner_f = jnp.exp((pos + 1.0) * lgf_l) * k_scale
    inner_b = jnp.exp((chunk - pos) * lgb_l) * k_scale
    ii = lax.broadcasted_iota(jnp.int32, (chunk, chunk), 0)
    jj = lax.broadcasted_iota(jnp.int32, (chunk, chunk), 1)
    d = (ii - jj).astype(F32)
    dmask = jnp.where(d >= 0, jnp.exp(jnp.maximum(d, 0.0) * lgf), jnp.exp(jnp.maximum(-d, 0.0) * lgb)) * k_scale

    def rows(c):
        return slice(c * chunk, (c + 1) * chunk)

    def chunk_kv(c):
        r = rows(c)
        k = _rope(k_ref[r, :].astype(F32), cos_ref[r, :], sin_ref[r, :])
        k_scr[r, :] = k.astype(BF16)
        v = v_ref[r, :].astype(F32)
        vz = jnp.concatenate([v * zeta_f, v * zeta_b], axis=1).astype(BF16)
        kt = k.T.astype(BF16)
        kv_scr[c] = jnp.dot(kt, vz, preferred_element_type=F32)

    for c in range(n_chunks):
        chunk_kv(c)

    dec_f = jnp.exp(chunk * lgf_l)
    dec_b = jnp.exp(chunk * lgb_l)
    sf = jnp.zeros((HEAD_DIM, LANES), F32)
    sb = jnp.zeros((HEAD_DIM, LANES), F32)
    for c in range(n_chunks):
        cb = n_chunks - 1 - c
        st_scr[c, :, :LANES] = sf.astype(BF16)
        st_scr[cb, :, LANES:] = sb.astype(BF16)
        sf = dec_f * sf + kv_scr[c, :, :LANES]
        sb = dec_b * sb + kv_scr[cb, :, LANES:]

    gn = gn_ref[...]

    def chunk_out(c):
        r = rows(c)
        q = _rope(q_ref[r, :].astype(F32), cos_ref[r, :], sin_ref[r, :]).astype(BF16)
        a = lax.dot_general(q, k_scr[r, :], (((1,), (1,)), ((), ())), preferred_element_type=F32) * dmask
        y = jnp.dot(a.astype(BF16), v_ref[r, :].astype(BF16), preferred_element_type=F32)
        cr = jnp.dot(q, st_scr[c], preferred_element_type=F32)
        y = y + cr[:, :LANES] * inner_f + cr[:, LANES:] * inner_b
        mu = jnp.mean(y, axis=-1, keepdims=True)
        yc = y - mu
        var = jnp.mean(yc * yc, axis=-1, keepdims=True)
        yn = yc * lax.rsqrt(var + EPS)
        g = g_ref[r, :].astype(F32)
        silu = g * (1.0 / (1.0 + jnp.exp(-g)))
        o_ref[r, :] = (yn * gn * silu).astype(o_ref.dtype)

    for c in range(n_chunks):
        chunk_out(c)


def _retention(proj, cos, sin, dec, gn, batch, seq):
    m = proj.shape[0]
    n_chunks = seq // RET_CHUNK
    col = lambda off: (lambda b, h: (b, off // LANES + h))
    return pl.pallas_call(
        functools.partial(_ret_kernel, chunk=RET_CHUNK),
        grid=(batch, RET_HEADS),
        in_specs=[
            pl.BlockSpec((seq, LANES), col(OFF_RQ)),
            pl.BlockSpec((seq, LANES), col(OFF_RK)),
            pl.BlockSpec((seq, LANES), col(OFF_RV)),
            pl.BlockSpec((seq, LANES), col(OFF_RG)),
            pl.BlockSpec((seq, LANES), lambda b, h: (0, 0)),
            pl.BlockSpec((seq, LANES), lambda b, h: (0, 0)),
            pl.BlockSpec((1, 2, RET_CHUNK), lambda b, h: (h, 0, 0)),
            pl.BlockSpec((1, LANES), lambda b, h: (0, h)),
        ],
        out_specs=pl.BlockSpec((seq, LANES), lambda b, h: (b, h)),
        out_shape=jax.ShapeDtypeStruct((m, RET_WIDTH), BF16),
        scratch_shapes=[
            pltpu.VMEM((n_chunks, HEAD_DIM, 2 * LANES), F32),
            pltpu.VMEM((n_chunks, HEAD_DIM, 2 * LANES), BF16),
            pltpu.VMEM((seq, HEAD_DIM), BF16),
        ],
        compiler_params=_params("parallel", "parallel"),
        name="retention",
    )(proj, proj, proj, proj, cos, sin, dec, gn)


def _attn_kernel(q_ref, k_ref, v_ref, o_ref, *, tk, rb, d_qk, kv_heads):
    seq = k_ref.shape[0]
    nr = q_ref.shape[0] // rb
    q_heads = q_ref.shape[1] // d_qk
    for hd in range(q_heads):
        kv = hd * kv_heads // q_heads
        kc = slice(kv * d_qk, (kv + 1) * d_qk)
        vc = slice(kv * V_AUG, (kv + 1) * V_AUG)
        for r in range(nr):
            rows = slice(r * rb, (r + 1) * rb)
            q = q_ref[rows, hd * d_qk:(hd + 1) * d_qk]
            m = acc = None
            for j in range(seq // tk):
                ks = slice(j * tk, (j + 1) * tk)
                s = lax.dot_general(q, k_ref[ks, kc], (((1,), (1,)), ((), ())), preferred_element_type=F32)
                mj = jnp.max(s, axis=-1, keepdims=True)
                if j == 0:
                    m = mj
                    acc = jnp.dot(jnp.exp2(s - m).astype(BF16), v_ref[ks, vc], preferred_element_type=F32)
                else:
                    m_new = jnp.maximum(m, mj)
                    alpha = jnp.exp2(m - m_new)
                    acc = alpha * acc + jnp.dot(jnp.exp2(s - m_new).astype(BF16), v_ref[ks, vc],
                                                preferred_element_type=F32)
                    m = m_new
            o_ref[rows, hd * LANES:(hd + 1) * LANES] = (acc[:, :LANES] / acc[:, LANES:]).astype(o_ref.dtype)


def _attention(q, k, v, batch, seq, heads, group, d_qk):
    m = q.shape[0]
    tq = min(seq, ROWS_ATTN)
    nq = seq // tq
    hps = ROWS_ATTN // tq
    kv_heads = 1 if group >= hps else hps // group
    d_v = HEAD_DIM
    return pl.pallas_call(
        functools.partial(_attn_kernel, tk=TK_LONG_ATTN if tq == ROWS_ATTN else TK_SHORT_ATTN, rb=RB_ATTN,
                          d_qk=d_qk, kv_heads=kv_heads),
        grid=(batch, heads // hps, nq),
        in_specs=[
            pl.BlockSpec((tq, hps * d_qk), lambda b, h, i: (b * nq + i, h)),
            pl.BlockSpec((seq, kv_heads * d_qk), lambda b, h, i: (b, h * hps // (group * kv_heads))),
            pl.BlockSpec((seq, kv_heads * V_AUG), lambda b, h, i: (b, h * hps // (group * kv_heads))),
        ],
        out_specs=pl.BlockSpec((tq, hps * d_v), lambda b, h, i: (b * nq + i, h)),
        out_shape=jax.ShapeDtypeStruct((m, heads * d_v), BF16),
        compiler_params=_params("parallel", "parallel", "parallel"),
        name="attention",
    )(q, k, v)


def _outproj_kernel(x_ref, a_ref, b_ref, c_ref, w_ref, wsrc_ref, o_ref, wdst_ref):
    wa = a_ref.shape[1]
    wb = b_ref.shape[1]
    acc = x_ref[...]
    acc = acc + jnp.dot(a_ref[...], w_ref[0:wa, :], preferred_element_type=F32)
    acc = acc + jnp.dot(b_ref[...], w_ref[wa:wa + wb, :], preferred_element_type=F32)
    acc = acc + jnp.dot(c_ref[...], w_ref[wa + wb:, :], preferred_element_type=F32)
    o_ref[...] = acc
    wdst_ref[...] = wsrc_ref[...].astype(BF16)


def _outproj(x, a, b, c, w, l, w_mlp):
    m = x.shape[0]
    tm = TM_PROJ
    steps = m // tm
    _, w_rows, w_cols = w_mlp.shape
    assert w_rows % steps == 0
    slab = w_rows // steps
    row = lambda width: pl.BlockSpec((tm, width), lambda i: (i, 0))
    return pl.pallas_call(
        _outproj_kernel,
        grid=(steps,),
        in_specs=[
            row(D_MODEL), row(a.shape[1]), row(b.shape[1]), row(c.shape[1]),
            pl.BlockSpec((None, D_MODEL, D_MODEL), lambda i: (l, 0, 0), pipeline_mode=pl.Buffered(1)),
            pl.BlockSpec((None, slab, w_cols), lambda i: (l, i, 0)),
        ],
        out_specs=[row(D_MODEL), pl.BlockSpec((slab, w_cols), lambda i: (i, 0))],
        out_shape=[jax.ShapeDtypeStruct((m, D_MODEL), F32), jax.ShapeDtypeStruct((w_rows, w_cols), BF16)],
        compiler_params=_params("parallel"),
        name="out_proj",
    )(x, a, b, c, w, w_mlp)


def _mlp_kernel(x_ref, g_ref, wu_ref, wd_ref, o_ref, h_scr):
    @pl.when(pl.program_id(1) == 0)
    def _():
        x = x_ref[...]
        h_scr[...] = _rms(x, g_ref[...]).astype(BF16)
        o_ref[...] = x

    u = jnp.dot(h_scr[...], wu_ref[...], preferred_element_type=F32)
    u = jnp.square(jnp.maximum(u, 0.0)).astype(BF16)
    o_ref[...] += jnp.dot(u, wd_ref[...], preferred_element_type=F32)


def _mlp(x, g, wu, wd):
    m = x.shape[0]
    tm = TM_MLP
    tf = TF_MLP
    return pl.pallas_call(
        _mlp_kernel,
        grid=(m // tm, D_FF // tf),
        in_specs=[
            pl.BlockSpec((tm, D_MODEL), lambda i, f: (i, 0)),
            pl.BlockSpec((1, D_MODEL), lambda i, f: (0, 0)),
            pl.BlockSpec((D_MODEL, tf), lambda i, f: (0, f)),
            pl.BlockSpec((tf, D_MODEL), lambda i, f: (f, 0)),
        ],
        out_specs=pl.BlockSpec((tm, D_MODEL), lambda i, f: (i, 0)),
        out_shape=jax.ShapeDtypeStruct((m, D_MODEL), F32),
        scratch_shapes=[pltpu.VMEM((tm, D_MODEL), BF16)],
        compiler_params=_params("parallel", "arbitrary"),
        name="mlp",
    )(x, g, wu, wd)


def _rope_tables(seq):
    t = jnp.arange(seq, dtype=F32)

    def half_tables(ang):
        c, s = jnp.cos(ang), jnp.sin(ang)
        return jnp.concatenate([c, c], axis=-1), jnp.concatenate([-s, s], axis=-1)

    inv = 1.0 / (ROPE_THETA ** (jnp.arange(0, HEAD_DIM, 2, dtype=F32) / HEAD_DIM))
    ret = half_tables(t[:, None] * inv[None, :])

    ti = jnp.arange(seq)
    row = (ti // GRID_W).astype(F32)
    colp = (ti % GRID_W).astype(F32)

    def axial(dim):
        half = dim // 2
        iv = 1.0 / (ROPE_THETA ** (jnp.arange(0, half, 2, dtype=F32) / half))
        return jnp.concatenate([row[:, None] * iv[None, :], colp[:, None] * iv[None, :]], axis=-1)

    ax128 = half_tables(axial(HEAD_DIM))
    a64 = axial(MLA_ROPE)
    c, s = jnp.cos(a64), jnp.sin(a64)
    z = jnp.zeros_like(c)
    ax64 = (jnp.concatenate([c, z, c, z], axis=-1), jnp.concatenate([-s, z, s, z], axis=-1))
    return ret, ax128, ax64


def _pad_rope_cols(w):
    z = jnp.zeros(w.shape[:-1] + (MLA_ROPE // 2,), w.dtype)
    return jnp.concatenate([w[..., :MLA_ROPE // 2], z, w[..., MLA_ROPE // 2:], z], axis=-1)


def _w_in_layout_kernel(w_ref, o_ref):
    half = MLA_ROPE // 2
    o_ref[:, :OFF_KROPE] = w_ref[:, :OFF_KROPE].astype(BF16)
    tail = w_ref[:, OFF_KROPE:]
    x = tail[:, :LANES]
    lane = lax.broadcasted_iota(jnp.int32, x.shape, 1)
    x2 = jnp.where(lane >= 2 * half, jnp.where(lane < 3 * half, pltpu.roll(x, half, 1), 0.0), 0.0)
    o_ref[:, OFF_KROPE:OFF_GQ] = (jnp.where(lane < half, x, 0.0) + x2).astype(BF16)
    o_ref[:, OFF_GQ:] = tail[:, MLA_ROPE:].astype(BF16)


def _w_in_layout(w_in):
    depth, rows, width = w_in.shape
    tr = 256
    return pl.pallas_call(
        _w_in_layout_kernel,
        grid=(depth, rows // tr),
        in_specs=[pl.BlockSpec((None, tr, width), lambda l, i: (l, i, 0))],
        out_specs=pl.BlockSpec((None, tr, IN_PAD), lambda l, i: (l, i, 0)),
        out_shape=jax.ShapeDtypeStruct((depth, rows, IN_PAD), BF16),
        compiler_params=_params("parallel", "parallel"),
        name="w_in_layout",
    )(w_in)


def _matmul_weights(w_in, mla_w_uq, mla_w_ukv, w_out):
    depth = w_in.shape[0]
    uq = mla_w_uq.reshape(depth, MLA_Q_RANK, MLA_HEADS, MLA_QK)
    uq_p = jnp.concatenate([uq[..., :MLA_NOPE], _pad_rope_cols(uq[..., MLA_NOPE:])], axis=-1)
    uq_p = uq_p.reshape(depth, MLA_Q_RANK, MLA_HEADS * MLA_PAD)
    return dict(w_in=_w_in_layout(w_in), wuq=uq_p.astype(BF16), wukv=mla_w_ukv.astype(BF16),
                w_out=w_out.astype(BF16))


def _layer_params(l, ln1_w, ret_decay_fwd, ret_decay_bwd, ret_gn_w, mla_q_a_norm, mla_kv_a_norm, mla_q_norm,
                  mla_k_norm, gqa_q_norm, gqa_k_norm, ln2_w):
    qn = mla_q_norm[l]
    gq = (jnp.concatenate([qn[:MLA_NOPE], _pad_rope_cols(qn[MLA_NOPE:])]) * (MLA_QK ** -0.5 * LOG2E)).reshape(1, MLA_PAD)
    kn = mla_k_norm[l]
    lg = jnp.stack([jax.nn.log_sigmoid(ret_decay_fwd[l].astype(F32)), jax.nn.log_sigmoid(ret_decay_bwd[l].astype(F32))],
                   axis=1)
    return dict(
        l=l,
        ln1=ln1_w[l].reshape(1, D_MODEL),
        dec=jnp.broadcast_to(lg[:, :, None], (RET_HEADS, 2, RET_CHUNK)),
        gn=ret_gn_w[l].reshape(1, RET_WIDTH),
        gqa_n=mla_q_a_norm[l].reshape(1, MLA_Q_RANK),
        gkva_n=mla_kv_a_norm[l].reshape(1, MLA_KV_RANK),
        gq=gq,
        gkn=kn[:MLA_NOPE].reshape(1, LANES),
        gkr=_pad_rope_cols(kn[MLA_NOPE:]).reshape(1, LANES),
        nq=(gqa_q_norm[l] * (HEAD_DIM ** -0.5 * LOG2E)).reshape(1, LANES),
        nk=gqa_k_norm[l].reshape(1, LANES),
        ln2=ln2_w[l].reshape(1, D_MODEL),
    )


def _mixer(x, w, p, tables, batch, seq, w_mlp):
    (ret_c, ret_s), (ax_c, ax_s), (m_c, m_s) = tables
    rqkvg, qm, km, vm, qg, kg, vg = _inproj(x, w, p, m_c, m_s, ax_c, ax_s, seq)
    ret = _retention(rqkvg, ret_c, ret_s, p["dec"], p["gn"], batch, seq)
    mla = _attention(qm, km, vm, batch, seq, MLA_HEADS, 1, MLA_PAD)
    gqa = _attention(qg, kg, vg, batch, seq, GQA_HEADS, GQA_GROUP, HEAD_DIM)
    return _outproj(x, ret, mla, gqa, w["w_out"], p["l"], w_mlp)


def kernel(x_prompt, x_sample, ln1_w, w_in, ret_decay_fwd, ret_decay_bwd, ret_gn_w, mla_q_a_norm, mla_w_uq,
           mla_kv_a_norm, mla_w_ukv, mla_q_norm, mla_k_norm, gqa_q_norm, gqa_k_norm, w_out, ln2_w, w_up, w_down):
    depth = w_in.shape[0]
    w = _matmul_weights(w_in, mla_w_uq, mla_w_ukv, w_out)
    groups = [x_prompt, x_sample]
    mlp_weights = [w_up, w_down]
    tables = [_rope_tables(x.shape[1]) for x in groups]
    ys = [x.reshape(x.shape[0] * x.shape[1], D_MODEL) for x in groups]
    for l in range(depth):
        p = _layer_params(l, ln1_w, ret_decay_fwd, ret_decay_bwd, ret_gn_w, mla_q_a_norm, mla_kv_a_norm, mla_q_norm,
                          mla_k_norm, gqa_q_norm, gqa_k_norm, ln2_w)
        mixed = [_mixer(y, w, p, tables[g], groups[g].shape[0], groups[g].shape[1], mlp_weights[g])
                 for g, y in enumerate(ys)]
        (_, wu), (_, wd) = mixed
        ys = [_mlp(x1, p["ln2"], wu, wd) for x1, _ in mixed]
    return tuple(y.reshape(x.shape) for y, x in zip(ys, groups))
```

```python
import functools

import jax
import jax.numpy as jnp
from jax import lax
from jax.experimental import pallas as pl
from jax.experimental.pallas import tpu as pltpu

D_MODEL = 2048
HEAD_DIM = 128
RET_HEADS = 4
RET_WIDTH = RET_HEADS * HEAD_DIM
MLA_HEADS = 4
MLA_Q_RANK = 384
MLA_KV_RANK = 256
MLA_NOPE = 128
MLA_ROPE = 64
MLA_QK = MLA_NOPE + MLA_ROPE
MLA_V = 128
GQA_HEADS = 8
GQA_KV_HEADS = 2
GQA_GROUP = GQA_HEADS // GQA_KV_HEADS
D_FF = 4 * D_MODEL
GRID_W = 64
ROPE_THETA = 10000.0
EPS = 1e-6
LOG2E = 1.4426950408889634

LANES = 128
MLA_PAD = 2 * LANES

OFF_RQ = 0
OFF_RK = 512
OFF_RV = 1024
OFF_RG = 1536
OFF_CQ = 2048
OFF_CKV = 2432
OFF_KROPE = 2688
OFF_GQ = 2752
OFF_GK = 3776
OFF_GV = 4032
IN_WIDTH = 4288
V_AUG = 2 * LANES

TM_PROJ = 512
TM_MLP = 512
TF_MLP = 1024
ROWS_ATTN = 4096
RB_ATTN = 512
TK_LONG_ATTN = 1024
TK_SHORT_ATTN = 256
RET_CHUNK = 256
VMEM_LIMIT = 52 * 1024 * 1024

F32 = jnp.float32
BF16 = jnp.bfloat16


def _params(*sem):
    return pltpu.CompilerParams(dimension_semantics=sem, vmem_limit_bytes=VMEM_LIMIT)


def _rms(x, g):
    ms = jnp.mean(x * x, axis=-1, keepdims=True)
    return x * lax.rsqrt(ms + EPS) * g


def _rope(x, c, s):
    return x * c + pltpu.roll(x, LANES // 2, 1) * s


def _inproj_kernel(x_ref, g_ref, w_ref, gqa_ref, gkva_ref, wuq_ref, wukv_ref, gq_ref, gkn_ref, gkr_ref, nq_ref, nk_ref,
                   mcos_ref, msin_ref, acos_ref, asin_ref, ret_out, qm_out, km_out, vm_out, qg_out, kg_out, vg_out):
    h = _rms(x_ref[...], g_ref[...]).astype(BF16)
    ones = jnp.ones((x_ref.shape[0], LANES), BF16)

    def proj_t(wt):
        return lax.dot_general(h, wt, (((1,), (1,)), ((), ())), preferred_element_type=F32)

    def proj(c0, c1):
        return proj_t(w_ref[c0:c1, :])

    half = MLA_ROPE // 2
    zrows = jnp.zeros((half, w_ref.shape[1]), BF16)
    pa = proj_t(jnp.concatenate([w_ref[OFF_CQ:OFF_KROPE + half, :], zrows,
                                 w_ref[OFF_KROPE + half:OFF_GQ, :], zrows], axis=0))
    kr = pa[:, OFF_KROPE - OFF_CQ:]
    pb = proj(OFF_GQ, OFF_GV)

    def cols(off, width):
        if off < OFF_KROPE:
            return pa[:, off - OFF_CQ:off - OFF_CQ + width]
        return pb[:, off - OFF_GQ:off - OFF_GQ + width]

    cqn = _rms(cols(OFF_CQ, MLA_Q_RANK), gqa_ref[...]).astype(BF16)
    qm = jnp.dot(cqn, wuq_ref[...], preferred_element_type=F32)
    ckvn = _rms(cols(OFF_CKV, MLA_KV_RANK), gkva_ref[...]).astype(BF16)
    kvm = jnp.dot(ckvn, wukv_ref[...], preferred_element_type=F32)
    cs = mcos_ref[...]
    sn = msin_ref[...]
    kr_ss =jnp.sum(kr * kr, axis=-1, keepdims=True)
    kr_rot = _rope(kr * gkr_ref[...], cs, sn)
    gq_n = gq_ref[:, :LANES]
    gq_r = gq_ref[:, LANES:]
    gk_n = gkn_ref[...]
    inv_d = 1.0 / MLA_QK
    for hd in range(MLA_HEADS):
        o = MLA_PAD * hd
        qn = qm[:, o:o + LANES]
        qr = qm[:, o + LANES:o + MLA_PAD]
        rq = lax.rsqrt((jnp.sum(qn * qn, axis=-1, keepdims=True) + jnp.sum(qr * qr, axis=-1, keepdims=True)) * inv_d
                       + EPS)
        qm_out[:, o:o + LANES] = (qn * rq * gq_n).astype(BF16)
        qm_out[:, o + LANES:o + MLA_PAD] = _rope(qr * rq * gq_r, cs, sn).astype(BF16)
        kn = kvm[:, o:o + LANES]
        rk = lax.rsqrt((jnp.sum(kn * kn, axis=-1, keepdims=True) + kr_ss) * inv_d + EPS)
        km_out[:, o:o + LANES] = (kn * rk * gk_n).astype(BF16)
        km_out[:, o + LANES:o + MLA_PAD] = (kr_rot * rk).astype(BF16)
        vm_out[:, o:o + LANES] = kvm[:, o + LANES:o + MLA_PAD].astype(BF16)
        vm_out[:, o + LANES:o + V_AUG] = ones

    cs = acos_ref[...]
    sn = asin_ref[...]
    nq = nq_ref[...]
    nk = nk_ref[...]
    for hd in range(GQA_HEADS):
        qg_out[:, LANES * hd:LANES * (hd + 1)] = _rope(_rms(cols(OFF_GQ + LANES * hd, LANES), nq), cs, sn).astype(BF16)
    for hd in range(GQA_KV_HEADS):
        kg_out[:, LANES * hd:LANES * (hd + 1)] = _rope(_rms(cols(OFF_GK + LANES * hd, LANES), nk), cs, sn).astype(BF16)

    gv = proj(OFF_GV, IN_WIDTH).astype(BF16)
    for hd in range(GQA_KV_HEADS):
        vg_out[:, V_AUG * hd:V_AUG * hd + LANES] = gv[:, LANES * hd:LANES * (hd + 1)]
        vg_out[:, V_AUG * hd + LANES:V_AUG * (hd + 1)] = ones

    for c0 in range(0, OFF_CQ, 1024):
        ret_out[:, c0:c0 + 1024] = proj(c0, c0 + 1024).astype(BF16)


def _inproj(x, w, p, mcos, msin, acos, asin, seq):
    m = x.shape[0]
    tm = TM_PROJ
    nseq = seq // tm
    l = p["l"]
    full = lambda shape: pl.BlockSpec(shape, lambda i: (0, 0))
    layer = lambda shape: pl.BlockSpec((None,) + shape, lambda i: (l, 0, 0))
    table = pl.BlockSpec((tm, LANES), lambda i: (i % nseq, 0))
    widths = (OFF_CQ, MLA_HEADS * MLA_PAD, MLA_HEADS * MLA_PAD, MLA_HEADS * V_AUG, GQA_HEADS * HEAD_DIM,
              GQA_KV_HEADS * HEAD_DIM, GQA_KV_HEADS * V_AUG)
    return pl.pallas_call(
        _inproj_kernel,
        grid=(m // tm,),
        in_specs=[
            pl.BlockSpec((tm, D_MODEL), lambda i: (i, 0)),
            full((1, D_MODEL)),
            pl.BlockSpec((None, IN_WIDTH, D_MODEL), lambda i: (l, 0, 0), pipeline_mode=pl.Buffered(1)),
            full((1, MLA_Q_RANK)),
            full((1, MLA_KV_RANK)),
            layer((MLA_Q_RANK, MLA_HEADS * MLA_PAD)),
            layer((MLA_KV_RANK, MLA_HEADS * MLA_PAD)),
            full((1, MLA_PAD)),
            full((1, LANES)),
            full((1, LANES)),
            full((1, LANES)),
            full((1, LANES)),
            table, table, table, table,
        ],
        out_specs=[pl.BlockSpec((tm, wd), lambda i: (i, 0)) for wd in widths],
        out_shape=[jax.ShapeDtypeStruct((m, wd), BF16) for wd in widths],
        compiler_params=_params("parallel"),
        name="in_proj",
    )(x, p["ln1"], w["w_in"], p["gqa_n"], p["gkva_n"], w["wuq"], w["wukv"], p["gq"], p["gkn"], p["gkr"], p["nq"],
      p["nk"], mcos, msin, acos, asin)


def _ret_kernel(q_ref, k_ref, v_ref, g_ref, cos_ref, sin_ref, dec_ref, gn_ref, o_ref, kv_scr, st_scr, k_scr, *, chunk):
    seq = q_ref.shape[0]
    n_chunks = seq // chunk
    lgf = dec_ref[0, 0:1, :]
    lgb = dec_ref[0, 1:2, :]
    lgf_l = lgf[:, :LANES]
    lgb_l = lgb[:, :LANES]
    k_scale = HEAD_DIM ** -0.5
    pos = lax.broadcasted_iota(jnp.int32, (chunk, LANES), 0).astype(F32)
    zeta_f = jnp.exp((chunk - 1.0 - pos) * lgf_l)
    zeta_b = jnp.exp(pos * lgb_l)
    inner_f = jnp.exp((pos + 1.0) * lgf_l) * k_scale
    inner_b = jnp.exp((chunk - pos) * lgb_l) * k_scale
    ii = lax.broadcasted_iota(jnp.int32, (chunk, chunk), 0)
    jj = lax.broadcasted_iota(jnp.int32, (chunk, chunk), 1)
    d = (ii - jj).astype(F32)
    dmask = jnp.where(d >= 0, jnp.exp(jnp.maximum(d, 0.0) * lgf), jnp.exp(jnp.maximum(-d, 0.0) * lgb)) * k_scale

    def rows(c):
        return slice(c * chunk, (c + 1) * chunk)

    def chunk_kv(c):
        r = rows(c)
        k = _rope(k_ref[r, :].astype(F32), cos_ref[r, :], sin_ref[r, :])
        k_scr[r, :] = k.astype(BF16)
        v = v_ref[r, :].astype(F32)
        vz = jnp.concatenate([v * zeta_f, v * zeta_b], axis=1).astype(BF16)
        kt = k.T.astype(BF16)
        kv_scr[c] = jnp.dot(kt, vz, preferred_element_type=F32)

    for c in range(n_chunks):
        chunk_kv(c)

    dec_f = jnp.exp(chunk * lgf_l)
    dec_b = jnp.exp(chunk * lgb_l)
    sf = jnp.zeros((HEAD_DIM, LANES), F32)
    sb = jnp.zeros((HEAD_DIM, LANES), F32)
    for c in range(n_chunks):
        cb = n_chunks - 1 - c
        st_scr[c, :, :LANES] = sf.astype(BF16)
        st_scr[cb, :, LANES:] = sb.astype(BF16)
        sf = dec_f * sf + kv_scr[c, :, :LANES]
        sb = dec_b * sb + kv_scr[cb, :, LANES:]

    gn = gn_ref[...]

    def chunk_out(c):
        r = rows(c)
        q = _rope(q_ref[r, :].astype(F32), cos_ref[r, :], sin_ref[r, :]).astype(BF16)
        a = lax.dot_general(q, k_scr[r, :], (((1,), (1,)), ((), ())), preferred_element_type=F32) * dmask
        y = jnp.dot(a.astype(BF16), v_ref[r, :].astype(BF16), preferred_element_type=F32)
        cr = jnp.dot(q, st_scr[c], preferred_element_type=F32)
        y = y + cr[:, :LANES] * inner_f + cr[:, LANES:] * inner_b
        mu = jnp.mean(y, axis=-1, keepdims=True)
        yc = y - mu
        var = jnp.mean(yc * yc, axis=-1, keepdims=True)
        yn = yc * lax.rsqrt(var + EPS)
        g = g_ref[r, :].astype(F32)
        silu = g * (1.0 / (1.0 + jnp.exp(-g)))
        o_ref[r, :] = (yn * gn * silu).astype(o_ref.dtype)

    for c in range(n_chunks):
        chunk_out(c)


def _retention(proj, cos, sin, dec, gn, batch, seq):
    m = proj.shape[0]
    n_chunks = seq // RET_CHUNK
    col = lambda off: (lambda b, h: (b, off // LANES + h))
    return pl.pallas_call(
        functools.partial(_ret_kernel, chunk=RET_CHUNK),
        grid=(batch, RET_HEADS),
        in_specs=[
            pl.BlockSpec((seq, LANES), col(OFF_RQ)),
            pl.BlockSpec((seq, LANES), col(OFF_RK)),
            pl.BlockSpec((seq, LANES), col(OFF_RV)),
            pl.BlockSpec((seq, LANES), col(OFF_RG)),
            pl.BlockSpec((seq, LANES), lambda b, h: (0, 0)),
            pl.BlockSpec((seq, LANES), lambda b, h: (0, 0)),
            pl.BlockSpec((1, 2, RET_CHUNK), lambda b, h: (h, 0, 0)),
            pl.BlockSpec((1, LANES), lambda b, h: (0, h)),
        ],
        out_specs=pl.BlockSpec((seq, LANES), lambda b, h: (b, h)),
        out_shape=jax.ShapeDtypeStruct((m, RET_WIDTH), BF16),
        scratch_shapes=[
            pltpu.VMEM((n_chunks, HEAD_DIM, 2 * LANES), F32),
            pltpu.VMEM((n_chunks, HEAD_DIM, 2 * LANES), BF16),
            pltpu.VMEM((seq, HEAD_DIM), BF16),
        ],
        compiler_params=_params("parallel", "parallel"),
        name="retention",
    )(proj, proj, proj, proj, cos, sin, dec, gn)


def _attn_kernel(q_ref, k_ref, v_ref, o_ref, *, tk, rb, d_qk, kv_heads):
    seq = k_ref.shape[0]
    nr = q_ref.shape[0] // rb
    q_heads = q_ref.shape[1] // d_qk
    for hd in range(q_heads):
        kv = hd * kv_heads // q_heads
        kc = slice(kv * d_qk, (kv + 1) * d_qk)
        vc = slice(kv * V_AUG, (kv + 1) * V_AUG)
        for r in range(nr):
            rows = slice(r * rb, (r + 1) * rb)
            q = q_ref[rows, hd * d_qk:(hd + 1) * d_qk]
            m = acc = None
            for j in range(seq // tk):
                ks = slice(j * tk, (j + 1) * tk)
                s = lax.dot_general(q, k_ref[ks, kc], (((1,), (1,)), ((), ())), preferred_element_type=F32)
                mj = jnp.max(s, axis=-1, keepdims=True)
                if j == 0:
                    m = mj
                    acc = jnp.dot(jnp.exp2(s - m).astype(BF16), v_ref[ks, vc], preferred_element_type=F32)
                else:
                    m_new = jnp.maximum(m, mj)
                    alpha = jnp.exp2(m - m_new)
                    acc = alpha * acc + jnp.dot(jnp.exp2(s - m_new).astype(BF16), v_ref[ks, vc],
                                                preferred_element_type=F32)
                    m = m_new
            o_ref[rows, hd * LANES:(hd + 1) * LANES] = (acc[:, :LANES] / acc[:, LANES:]).astype(o_ref.dtype)


def _attention(q, k, v, batch, seq, heads, group, d_qk):
    m = q.shape[0]
    tq = min(seq, ROWS_ATTN)
    nq = seq // tq
    hps = ROWS_ATTN // tq
    kv_heads = 1 if group >= hps else hps // group
    d_v = HEAD_DIM
    return pl.pallas_call(
        functools.partial(_attn_kernel, tk=TK_LONG_ATTN if tq == ROWS_ATTN else TK_SHORT_ATTN, rb=RB_ATTN,
                          d_qk=d_qk, kv_heads=kv_heads),
        grid=(batch, heads // hps, nq),
        in_specs=[
            pl.BlockSpec((tq, hps * d_qk), lambda b, h, i: (b * nq + i, h)),
            pl.BlockSpec((seq, kv_heads * d_qk), lambda b, h, i: (b, h * hps // (group * kv_heads))),
            pl.BlockSpec((seq, kv_heads * V_AUG), lambda b, h, i: (b, h * hps // (group * kv_heads))),
        ],
        out_specs=pl.BlockSpec((tq, hps * d_v), lambda b, h, i: (b * nq + i, h)),
        out_shape=jax.ShapeDtypeStruct((m, heads * d_v), BF16),
        compiler_params=_params("parallel", "parallel", "parallel"),
        name="attention",
    )(q, k, v)


def _outproj_kernel(x_ref, a_ref, b_ref, c_ref, w_ref, wsrc_ref, o_ref, wdst_ref):
    wa = a_ref.shape[1]
    wb = b_ref.shape[1]
    acc = x_ref[...]
    acc = acc + jnp.dot(a_ref[...], w_ref[0:wa, :], preferred_element_type=F32)
    acc = acc + jnp.dot(b_ref[...], w_ref[wa:wa + wb, :], preferred_element_type=F32)
    acc = acc + jnp.dot(c_ref[...], w_ref[wa + wb:, :], preferred_element_type=F32)
    o_ref[...] = acc
    wdst_ref[...] = wsrc_ref[...].astype(BF16)


def _outproj(x, a, b, c, w, l, w_mlp):
    m = x.shape[0]
    tm = TM_PROJ
    steps = m // tm
    _, w_rows, w_cols = w_mlp.shape
    assert w_rows % steps == 0
    slab = w_rows // steps
    row = lambda width: pl.BlockSpec((tm, width), lambda i: (i, 0))
    return pl.pallas_call(
        _outproj_kernel,
        grid=(steps,),
        in_specs=[
            row(D_MODEL), row(a.shape[1]), row(b.shape[1]), row(c.shape[1]),
            pl.BlockSpec((None, D_MODEL, D_MODEL), lambda i: (l, 0, 0), pipeline_mode=pl.Buffered(1)),
            pl.BlockSpec((None, slab, w_cols), lambda i: (l, i, 0)),
        ],
        out_specs=[row(D_MODEL), pl.BlockSpec((slab, w_cols), lambda i: (i, 0))],
        out_shape=[jax.ShapeDtypeStruct((m, D_MODEL), F32), jax.ShapeDtypeStruct((w_rows, w_cols), BF16)],
        compiler_params=_params("parallel"),
        name="out_proj",
    )(x, a, b, c, w, w_mlp)


def _mlp_kernel(x_ref, g_ref, wu_ref, wd_ref, o_ref, h_scr):
    @pl.when(pl.program_id(1) == 0)
    def _():
        x = x_ref[...]
        h_scr[...] = _rms(x, g_ref[...]).astype(BF16)
        o_ref[...] = x

    u = jnp.dot(h_scr[...], wu_ref[...], preferred_element_type=F32)
    u = jnp.square(jnp.maximum(u, 0.0)).astype(BF16)
    o_ref[...] += jnp.dot(u, wd_ref[...], preferred_element_type=F32)


def _mlp(x, g, wu, wd):
    m = x.shape[0]
    tm = TM_MLP
    tf = TF_MLP
    return pl.pallas_call(
        _mlp_kernel,
        grid=(m // tm, D_FF // tf),
        in_specs=[
            pl.BlockSpec((tm, D_MODEL), lambda i, f: (i, 0)),
            pl.BlockSpec((1, D_MODEL), lambda i, f: (0, 0)),
            pl.BlockSpec((D_MODEL, tf), lambda i, f: (0, f)),
            pl.BlockSpec((tf, D_MODEL), lambda i, f: (f, 0)),
        ],
        out_specs=pl.BlockSpec((tm, D_MODEL), lambda i, f: (i, 0)),
        out_shape=jax.ShapeDtypeStruct((m, D_MODEL), F32),
        scratch_shapes=[pltpu.VMEM((tm, D_MODEL), BF16)],
        compiler_params=_params("parallel", "arbitrary"),
        name="mlp",
    )(x, g, wu, wd)


def _rope_tables(seq):
    t = jnp.arange(seq, dtype=F32)

    def half_tables(ang):
        c, s = jnp.cos(ang), jnp.sin(ang)
        return jnp.concatenate([c, c], axis=-1), jnp.concatenate([-s, s], axis=-1)

    inv = 1.0 / (ROPE_THETA ** (jnp.arange(0, HEAD_DIM, 2, dtype=F32) / HEAD_DIM))
    ret = half_tables(t[:, None] * inv[None, :])

    ti = jnp.arange(seq)
    row = (ti // GRID_W).astype(F32)
    colp = (ti % GRID_W).astype(F32)

    def axial(dim):
        half = dim // 2
        iv = 1.0 / (ROPE_THETA ** (jnp.arange(0, half, 2, dtype=F32) / half))
        return jnp.concatenate([row[:, None] * iv[None, :], colp[:, None] * iv[None, :]], axis=-1)

    ax128 = half_tables(axial(HEAD_DIM))
    a64 = axial(MLA_ROPE)
    c, s = jnp.cos(a64), jnp.sin(a64)
    z = jnp.zeros_like(c)
    ax64 = (jnp.concatenate([c, z, c, z], axis=-1), jnp.concatenate([-s, z, s, z], axis=-1))
    return ret, ax128, ax64


def _pad_rope_cols(w):
    z = jnp.zeros(w.shape[:-1] + (MLA_ROPE // 2,), w.dtype)
    return jnp.concatenate([w[..., :MLA_ROPE // 2], z, w[..., MLA_ROPE // 2:], z], axis=-1)


def _matmul_weights(w_in, mla_w_uq, mla_w_ukv, w_out):
    depth = w_in.shape[0]
    w_in_t = jnp.swapaxes(w_in, 1, 2).astype(BF16)
    uq = mla_w_uq.reshape(depth, MLA_Q_RANK, MLA_HEADS, MLA_QK)
    uq_p = jnp.concatenate([uq[..., :MLA_NOPE], _pad_rope_cols(uq[..., MLA_NOPE:])], axis=-1)
    uq_p = uq_p.reshape(depth, MLA_Q_RANK, MLA_HEADS * MLA_PAD)
    return dict(w_in=w_in_t, wuq=uq_p.astype(BF16), wukv=mla_w_ukv.astype(BF16),
                w_out=w_out.astype(BF16))


def _layer_params(l, ln1_w, ret_decay_fwd, ret_decay_bwd, ret_gn_w, mla_q_a_norm, mla_kv_a_norm, mla_q_norm,
                  mla_k_norm, gqa_q_norm, gqa_k_norm, ln2_w):
    qn = mla_q_norm[l]
    gq = (jnp.concatenate([qn[:MLA_NOPE], _pad_rope_cols(qn[MLA_NOPE:])]) * (MLA_QK ** -0.5 * LOG2E)).reshape(1, MLA_PAD)
    kn = mla_k_norm[l]
    lg = jnp.stack([jax.nn.log_sigmoid(ret_decay_fwd[l].astype(F32)), jax.nn.log_sigmoid(ret_decay_bwd[l].astype(F32))],
                   axis=1)
    return dict(
        l=l,
        ln1=ln1_w[l].reshape(1, D_MODEL),
        dec=jnp.broadcast_to(lg[:, :, None], (RET_HEADS, 2, RET_CHUNK)),
        gn=ret_gn_w[l].reshape(1, RET_WIDTH),
        gqa_n=mla_q_a_norm[l].reshape(1, MLA_Q_RANK),
        gkva_n=mla_kv_a_norm[l].reshape(1, MLA_KV_RANK),
        gq=gq,
        gkn=kn[:MLA_NOPE].reshape(1, LANES),
        gkr=_pad_rope_cols(kn[MLA_NOPE:]).reshape(1, LANES),
        nq=(gqa_q_norm[l] * (HEAD_DIM ** -0.5 * LOG2E)).reshape(1, LANES),
        nk=gqa_k_norm[l].reshape(1, LANES),
        ln2=ln2_w[l].reshape(1, D_MODEL),
    )


def _mixer(x, w, p, tables, batch, seq, w_mlp):
    (ret_c, ret_s), (ax_c, ax_s), (m_c, m_s) = tables
    rqkvg, qm, km, vm, qg, kg, vg = _inproj(x, w, p, m_c, m_s, ax_c, ax_s, seq)
    ret = _retention(rqkvg, ret_c, ret_s, p["dec"], p["gn"], batch, seq)
    mla = _attention(qm, km, vm, batch, seq, MLA_HEADS, 1, MLA_PAD)
    gqa = _attention(qg, kg, vg, batch, seq, GQA_HEADS, GQA_GROUP, HEAD_DIM)
    return _outproj(x, ret, mla, gqa, w["w_out"], p["l"], w_mlp)


def kernel(x_prompt, x_sample, ln1_w, w_in, ret_decay_fwd, ret_decay_bwd, ret_gn_w, mla_q_a_norm, mla_w_uq,
           mla_kv_a_norm, mla_w_ukv, mla_q_norm, mla_k_norm, gqa_q_norm, gqa_k_norm, w_out, ln2_w, w_up, w_down):
    depth = w_in.shape[0]
    w = _matmul_weights(w_in, mla_w_uq, mla_w_ukv, w_out)
    groups = [x_prompt, x_sample]
    mlp_weights = [w_up, w_down]
    tables = [_rope_tables(x.shape[1]) for x in groups]
    ys = [x.reshape(x.shape[0] * x.shape[1], D_MODEL) for x in groups]
    for l in range(depth):
        p = _layer_params(l, ln1_w, ret_decay_fwd, ret_decay_bwd, ret_gn_w, mla_q_a_norm, mla_kv_a_norm, mla_q_norm,
                          mla_k_norm, gqa_q_norm, gqa_k_norm, ln2_w)
        mixed = [_mixer(y, w, p, tables[g], groups[g].shape[0], groups[g].shape[1], mlp_weights[g])
                 for g, y in enumerate(ys)]
        (_, wu), (_, wd) = mixed
        ys = [_mlp(x1, p["ln2"], wu, wd) for x1, _ in mixed]
    return tuple(y.reshape(x.shape) for y, x in zip(ys, groups))
```

```python
import functools

import jax
import jax.numpy as jnp
from jax import lax
from jax.experimental import pallas as pl
from jax.experimental.pallas import tpu as pltpu

D_MODEL = 2048
HEAD_DIM = 128
RET_HEADS = 4
RET_WIDTH = RET_HEADS * HEAD_DIM
MLA_HEADS = 4
MLA_Q_RANK = 384
MLA_KV_RANK = 256
MLA_NOPE = 128
MLA_ROPE = 64
MLA_QK = MLA_NOPE + MLA_ROPE
MLA_V = 128
GQA_HEADS = 8
GQA_KV_HEADS = 2
GQA_GROUP = GQA_HEADS // GQA_KV_HEADS
D_FF = 4 * D_MODEL
GRID_W = 64
ROPE_THETA = 10000.0
EPS = 1e-6
LOG2E = 1.4426950408889634

LANES = 128
MLA_PAD = 2 * LANES

OFF_RQ = 0
OFF_RK = 512
OFF_RV = 1024
OFF_RG = 1536
OFF_CQ = 2048
OFF_CKV = 2432
OFF_KROPE = 2688
OFF_GQ = 2752
OFF_GK = 3776
OFF_GV = 4032
IN_WIDTH = 4288
V_AUG = 2 * LANES

TM_PROJ = 512
TM_MLP = 512
TF_MLP = 1024
ROWS_ATTN = 4096
RB_ATTN = 512
TK_LONG_ATTN = 1024
TK_SHORT_ATTN = 256
RET_CHUNK = 256
VMEM_LIMIT = 52 * 1024 * 1024

F32 = jnp.float32
BF16 = jnp.bfloat16


def _params(*sem):
    return pltpu.CompilerParams(dimension_semantics=sem, vmem_limit_bytes=VMEM_LIMIT)


def _rms(x, g):
    ms = jnp.mean(x * x, axis=-1, keepdims=True)
    return x * lax.rsqrt(ms + EPS) * g


def _rope(x, c, s):
    return x * c + pltpu.roll(x, LANES // 2, 1) * s


def _inproj_kernel(x_ref, g_ref, w_ref, gqa_ref, gkva_ref, wuq_ref, wukv_ref, gq_ref, gkn_ref, gkr_ref, nq_ref, nk_ref,
                   mcos_ref, msin_ref, acos_ref, asin_ref, ret_out, qm_out, km_out, vm_out, qg_out, kg_out, vg_out):
    h = _rms(x_ref[...], g_ref[...]).astype(BF16)
    ones = jnp.ones((x_ref.shape[0], LANES), BF16)

    def proj_t(wt):
        return lax.dot_general(h, wt, (((1,), (1,)), ((), ())), preferred_element_type=F32)

    def proj(c0, c1):
        return proj_t(w_ref[c0:c1, :])

    half = MLA_ROPE // 2
    zrows = jnp.zeros((half, w_ref.shape[1]), BF16)
    pa = proj_t(jnp.concatenate([w_ref[OFF_CQ:OFF_KROPE + half, :], zrows,
                                 w_ref[OFF_KROPE + half:OFF_GQ, :], zrows], axis=0))
    kr = pa[:, OFF_KROPE - OFF_CQ:]
    pb = proj(OFF_GQ, OFF_GV)

    def cols(off, width):
        if off < OFF_KROPE:
            return pa[:, off - OFF_CQ:off - OFF_CQ + width]
        return pb[:, off - OFF_GQ:off - OFF_GQ + width]

    cqn = _rms(cols(OFF_CQ, MLA_Q_RANK), gqa_ref[...]).astype(BF16)
    qm = jnp.dot(cqn, wuq_ref[...], preferred_element_type=F32)
    ckvn = _rms(cols(OFF_CKV, MLA_KV_RANK), gkva_ref[...]).astype(BF16)
    kvm = jnp.dot(ckvn, wukv_ref[...], preferred_element_type=F32)
    cs = mcos_ref[...]
    sn = msin_ref[...]
    kr_ss =jnp.sum(kr * kr, axis=-1, keepdims=True)
    kr_rot = _rope(kr * gkr_ref[...], cs, sn)
    gq_n = gq_ref[:, :LANES]
    gq_r = gq_ref[:, LANES:]
    gk_n = gkn_ref[...]
    inv_d = 1.0 / MLA_QK
    for hd in range(MLA_HEADS):
        o = MLA_PAD * hd
        qn = qm[:, o:o + LANES]
        qr = qm[:, o + LANES:o + MLA_PAD]
        rq = lax.rsqrt((jnp.sum(qn * qn, axis=-1, keepdims=True) + jnp.sum(qr * qr, axis=-1, keepdims=True)) * inv_d
                       + EPS)
        qm_out[:, o:o + LANES] = (qn * rq * gq_n).astype(BF16)
        qm_out[:, o + LANES:o + MLA_PAD] = _rope(qr * rq * gq_r, cs, sn).astype(BF16)
        kn = kvm[:, o:o + LANES]
        rk = lax.rsqrt((jnp.sum(kn * kn, axis=-1, keepdims=True) + kr_ss) * inv_d + EPS)
        km_out[:, o:o + LANES] = (kn * rk * gk_n).astype(BF16)
        km_out[:, o + LANES:o + MLA_PAD] = (kr_rot * rk).astype(BF16)
        vm_out[:, o:o + LANES] = kvm[:, o + LANES:o + MLA_PAD].astype(BF16)
        vm_out[:, o + LANES:o + V_AUG] = ones

    cs = acos_ref[...]
    sn = asin_ref[...]
    nq = nq_ref[...]
    nk = nk_ref[...]
    for hd in range(GQA_HEADS):
        qg_out[:, LANES * hd:LANES * (hd + 1)] = _rope(_rms(cols(OFF_GQ + LANES * hd, LANES), nq), cs, sn).astype(BF16)
    for hd in range(GQA_KV_HEADS):
        kg_out[:, LANES * hd:LANES * (hd + 1)] = _rope(_rms(cols(OFF_GK + LANES * hd, LANES), nk), cs, sn).astype(BF16)

    gv = proj(OFF_GV, IN_WIDTH).astype(BF16)
    for hd in range(GQA_KV_HEADS):
        vg_out[:, V_AUG * hd:V_AUG * hd + LANES] = gv[:, LANES * hd:LANES * (hd + 1)]
        vg_out[:, V_AUG * hd + LANES:V_AUG * (hd + 1)] = ones

    for c0 in range(0, OFF_CQ, 1024):
        ret_out[:, c0:c0 + 1024] = proj(c0, c0 + 1024).astype(BF16)


def _inproj(x, w, p, mcos, msin, acos, asin, seq):
    m = x.shape[0]
    tm = TM_PROJ
    nseq = seq // tm
    l = p["l"]
    full = lambda shape: pl.BlockSpec(shape, lambda i: (0, 0))
    layer = lambda shape: pl.BlockSpec((None,) + shape, lambda i: (l, 0, 0))
    table = pl.BlockSpec((tm, LANES), lambda i: (i % nseq, 0))
    widths = (OFF_CQ, MLA_HEADS * MLA_PAD, MLA_HEADS * MLA_PAD, MLA_HEADS * V_AUG, GQA_HEADS * HEAD_DIM,
              GQA_KV_HEADS * HEAD_DIM, GQA_KV_HEADS * V_AUG)
    return pl.pallas_call(
        _inproj_kernel,
        grid=(m // tm,),
        in_specs=[
            pl.BlockSpec((tm, D_MODEL), lambda i: (i, 0)),
            full((1, D_MODEL)),
            pl.BlockSpec((None, IN_WIDTH, D_MODEL), lambda i: (l, 0, 0), pipeline_mode=pl.Buffered(1)),
            full((1, MLA_Q_RANK)),
            full((1, MLA_KV_RANK)),
            layer((MLA_Q_RANK, MLA_HEADS * MLA_PAD)),
            layer((MLA_KV_RANK, MLA_HEADS * MLA_PAD)),
            full((1, MLA_PAD)),
            full((1, LANES)),
            full((1, LANES)),
            full((1, LANES)),
            full((1, LANES)),
            table, table, table, table,
        ],
        out_specs=[pl.BlockSpec((tm, wd), lambda i: (i, 0)) for wd in widths],
        out_shape=[jax.ShapeDtypeStruct((m, wd), BF16) for wd in widths],
        compiler_params=_params("parallel"),
        name="in_proj",
    )(x, p["ln1"], w["w_in"], p["gqa_n"], p["gkva_n"], w["wuq"], w["wukv"], p["gq"], p["gkn"], p["gkr"], p["nq"],
      p["nk"], mcos, msin, acos, asin)


def _ret_kernel(q_ref, k_ref, v_ref, g_ref, cos_ref, sin_ref, dec_ref, gn_ref, o_ref, kv_scr, st_scr, k_scr, *, chunk):
    seq = q_ref.shape[0]
    n_chunks = seq // chunk
    lgf = dec_ref[0, 0:1, :]
    lgb = dec_ref[0, 1:2, :]
    lgf_l = lgf[:, :LANES]
    lgb_l = lgb[:, :LANES]
    k_scale = HEAD_DIM ** -0.5
    pos = lax.broadcasted_iota(jnp.int32, (chunk, LANES), 0).astype(F32)
    zeta_f = jnp.exp((chunk - 1.0 - pos) * lgf_l)
    zeta_b = jnp.exp(pos * lgb_l)
    inner_f = jnp.exp((pos + 1.0) * lgf_l) * k_scale
    inner_b = jnp.exp((chunk - pos) * lgb_l) * k_scale
    ii = lax.broadcasted_iota(jnp.int32, (chunk, chunk), 0)
    jj = lax.broadcasted_iota(jnp.int32, (chunk, chunk), 1)
    d = (ii - jj).astype(F32)
    dmask = jnp.where(d >= 0, jnp.exp(jnp.maximum(d, 0.0) * lgf), jnp.exp(jnp.maximum(-d, 0.0) * lgb)) * k_scale

    def rows(c):
        return slice(c * chunk, (c + 1) * chunk)

    def chunk_kv(c):
        r = rows(c)
        k = _rope(k_ref[r, :].astype(F32), cos_ref[r, :], sin_ref[r, :])
        k_scr[r, :] = k.astype(BF16)
        v = v_ref[r, :].astype(F32)
        vz = jnp.concatenate([v * zeta_f, v * zeta_b], axis=1).astype(BF16)
        kt = k.T.astype(BF16)
        kv_scr[c] = jnp.dot(kt, vz, preferred_element_type=F32)

    for c in range(n_chunks):
        chunk_kv(c)

    dec_f = jnp.exp(chunk * lgf_l)
    dec_b = jnp.exp(chunk * lgb_l)
    sf = jnp.zeros((HEAD_DIM, LANES), F32)
    sb = jnp.zeros((HEAD_DIM, LANES), F32)
    for c in range(n_chunks):
        cb = n_chunks - 1 - c
        st_scr[c, :, :LANES] = sf.astype(BF16)
        st_scr[cb, :, LANES:] = sb.astype(BF16)
        sf = dec_f * sf + kv_scr[c, :, :LANES]
        sb = dec_b * sb + kv_scr[cb, :, LANES:]

    gn = gn_ref[...]

    def chunk_out(c):
        r = rows(c)
        q = _rope(q_ref[r, :].astype(F32), cos_ref[r, :], sin_ref[r, :]).astype(BF16)
        a = lax.dot_general(q, k_scr[r, :], (((1,), (1,)), ((), ())), preferred_element_type=F32) * dmask
        y = jnp.dot(a.astype(BF16), v_ref[r, :].astype(BF16), preferred_element_type=F32)
        cr = jnp.dot(q, st_scr[c], preferred_element_type=F32)
        y = y + cr[:, :LANES] * inner_f + cr[:, LANES:] * inner_b
        mu = jnp.mean(y, axis=-1, keepdims=True)
        yc = y - mu
        var = jnp.mean(yc * yc, axis=-1, keepdims=True)
        yn = yc * lax.rsqrt(var + EPS)
        g = g_ref[r, :].astype(F32)
        silu = g * (1.0 / (1.0 + jnp.exp(-g)))
        o_ref[r, :] = (yn * gn * silu).astype(o_ref.dtype)

    for c in range(n_chunks):
        chunk_out(c)


def _retention(proj, cos, sin, dec, gn, batch, seq):
    m = proj.shape[0]
    n_chunks = seq // RET_CHUNK
    col = lambda off: (lambda b, h: (b, off // LANES + h))
    return pl.pallas_call(
        functools.partial(_ret_kernel, chunk=RET_CHUNK),
        grid=(batch, RET_HEADS),
        in_specs=[
            pl.BlockSpec((seq, LANES), col(OFF_RQ)),
            pl.BlockSpec((seq, LANES), col(OFF_RK)),
            pl.BlockSpec((seq, LANES), col(OFF_RV)),
            pl.BlockSpec((seq, LANES), col(OFF_RG)),
            pl.BlockSpec((seq, LANES), lambda b, h: (0, 0)),
            pl.BlockSpec((seq, LANES), lambda b, h: (0, 0)),
            pl.BlockSpec((1, 2, RET_CHUNK), lambda b, h: (h, 0, 0)),
            pl.BlockSpec((1, LANES), lambda b, h: (0, h)),
        ],
        out_specs=pl.BlockSpec((seq, LANES), lambda b, h: (b, h)),
        out_shape=jax.ShapeDtypeStruct((m, RET_WIDTH), BF16),
        scratch_shapes=[
            pltpu.VMEM((n_chunks, HEAD_DIM, 2 * LANES), F32),
            pltpu.VMEM((n_chunks, HEAD_DIM, 2 * LANES), BF16),
            pltpu.VMEM((seq, HEAD_DIM), BF16),
        ],
        compiler_params=_params("parallel", "parallel"),
        name="retention",
    )(proj, proj, proj, proj, cos, sin, dec, gn)


def _attn_kernel(q_ref, k_ref, v_ref, o_ref, *, tk, rb, d_qk, kv_heads):
    seq = k_ref.shape[0]
    nr = q_ref.shape[0] // rb
    q_heads = q_ref.shape[1] // d_qk
    for hd in range(q_heads):
        kv = hd * kv_heads // q_heads
        kc = slice(kv * d_qk, (kv + 1) * d_qk)
        vc = slice(kv * V_AUG, (kv + 1) * V_AUG)
        for r in range(nr):
            rows = slice(r * rb, (r + 1) * rb)
            q = q_ref[rows, hd * d_qk:(hd + 1) * d_qk]
            m = acc = None
            for j in range(seq // tk):
                ks = slice(j * tk, (j + 1) * tk)
                s = lax.dot_general(q, k_ref[ks, kc], (((1,), (1,)), ((), ())), preferred_element_type=F32)
                mj = jnp.max(s, axis=-1, keepdims=True)
                if j == 0:
                    m = mj
                    acc = jnp.dot(jnp.exp2(s - m).astype(BF16), v_ref[ks, vc], preferred_element_type=F32)
                else:
                    m_new = jnp.maximum(m, mj)
                    alpha = jnp.exp2(m - m_new)
                    acc = alpha * acc + jnp.dot(jnp.exp2(s - m_new).astype(BF16), v_ref[ks, vc],
                                                preferred_element_type=F32)
                    m = m_new
            o_ref[rows, hd * LANES:(hd + 1) * LANES] = (acc[:, :LANES] / acc[:, LANES:]).astype(o_ref.dtype)


def _attention(q, k, v, batch, seq, heads, group, d_qk):
    m = q.shape[0]
    tq = min(seq, ROWS_ATTN)
    nq = seq // tq
    hps = ROWS_ATTN // tq if group == 1 else 1
    kv_heads = 1 if group >= hps else hps // group
    d_v = HEAD_DIM
    return pl.pallas_call(
        functools.partial(_attn_kernel, tk=TK_LONG_ATTN if tq == ROWS_ATTN else TK_SHORT_ATTN, rb=RB_ATTN,
                          d_qk=d_qk, kv_heads=kv_heads),
        grid=(batch, heads // hps, nq),
        in_specs=[
            pl.BlockSpec((tq, hps * d_qk), lambda b, h, i: (b * nq + i, h)),
            pl.BlockSpec((seq, kv_heads * d_qk), lambda b, h, i: (b, h * hps // (group * kv_heads))),
            pl.BlockSpec((seq, kv_heads * V_AUG), lambda b, h, i: (b, h * hps // (group * kv_heads))),
        ],
        out_specs=pl.BlockSpec((tq, hps * d_v), lambda b, h, i: (b * nq + i, h)),
        out_shape=jax.ShapeDtypeStruct((m, heads * d_v), BF16),
        compiler_params=_params("parallel", "parallel", "parallel"),
        name="attention",
    )(q, k, v)


def _outproj_kernel(x_ref, a_ref, b_ref, c_ref, w_ref, wsrc_ref, o_ref, wdst_ref):
    wa = a_ref.shape[1]
    wb = b_ref.shape[1]
    acc = x_ref[...]
    acc = acc + jnp.dot(a_ref[...], w_ref[0:wa, :], preferred_element_type=F32)
    acc = acc + jnp.dot(b_ref[...], w_ref[wa:wa + wb, :], preferred_element_type=F32)
    acc = acc + jnp.dot(c_ref[...], w_ref[wa + wb:, :], preferred_element_type=F32)
    o_ref[...] = acc
    wdst_ref[...] = wsrc_ref[...].astype(BF16)


def _outproj(x, a, b, c, w, l, w_mlp):
    m = x.shape[0]
    tm = TM_PROJ
    steps = m // tm
    _, w_rows, w_cols = w_mlp.shape
    assert w_rows % steps == 0
    slab = w_rows // steps
    row = lambda width: pl.BlockSpec((tm, width), lambda i: (i, 0))
    return pl.pallas_call(
        _outproj_kernel,
        grid=(steps,),
        in_specs=[
            row(D_MODEL), row(a.shape[1]), row(b.shape[1]), row(c.shape[1]),
            pl.BlockSpec((None, D_MODEL, D_MODEL), lambda i: (l, 0, 0), pipeline_mode=pl.Buffered(1)),
            pl.BlockSpec((None, slab, w_cols), lambda i: (l, i, 0)),
        ],
        out_specs=[row(D_MODEL), pl.BlockSpec((slab, w_cols), lambda i: (i, 0))],
        out_shape=[jax.ShapeDtypeStruct((m, D_MODEL), F32), jax.ShapeDtypeStruct((w_rows, w_cols), BF16)],
        compiler_params=_params("parallel"),
        name="out_proj",
    )(x, a, b, c, w, w_mlp)


def _mlp_kernel(x_ref, g_ref, wu_ref, wd_ref, o_ref, h_scr):
    @pl.when(pl.program_id(1) == 0)
    def _():
        x = x_ref[...]
        h_scr[...] = _rms(x, g_ref[...]).astype(BF16)
        o_ref[...] = x

    u = jnp.dot(h_scr[...], wu_ref[...], preferred_element_type=F32)
    u = jnp.square(jnp.maximum(u, 0.0)).astype(BF16)
    o_ref[...] += jnp.dot(u, wd_ref[...], preferred_element_type=F32)


def _mlp(x, g, wu, wd):
    m = x.shape[0]
    tm = TM_MLP
    tf = TF_MLP
    return pl.pallas_call(
        _mlp_kernel,
        grid=(m // tm, D_FF // tf),
        in_specs=[
            pl.BlockSpec((tm, D_MODEL), lambda i, f: (i, 0)),
            pl.BlockSpec((1, D_MODEL), lambda i, f: (0, 0)),
            pl.BlockSpec((D_MODEL, tf), lambda i, f: (0, f)),
            pl.BlockSpec((tf, D_MODEL), lambda i, f: (f, 0)),
        ],
        out_specs=pl.BlockSpec((tm, D_MODEL), lambda i, f: (i, 0)),
        out_shape=jax.ShapeDtypeStruct((m, D_MODEL), F32),
        scratch_shapes=[pltpu.VMEM((tm, D_MODEL), BF16)],
        compiler_params=_params("parallel", "arbitrary"),
        name="mlp",
    )(x, g, wu, wd)


def _rope_tables(seq):
    t = jnp.arange(seq, dtype=F32)

    def half_tables(ang):
        c, s = jnp.cos(ang), jnp.sin(ang)
        return jnp.concatenate([c, c], axis=-1), jnp.concatenate([-s, s], axis=-1)

    inv = 1.0 / (ROPE_THETA ** (jnp.arange(0, HEAD_DIM, 2, dtype=F32) / HEAD_DIM))
    ret = half_tables(t[:, None] * inv[None, :])

    ti = jnp.arange(seq)
    row = (ti // GRID_W).astype(F32)
    colp = (ti % GRID_W).astype(F32)

    def axial(dim):
        half = dim // 2
        iv = 1.0 / (ROPE_THETA ** (jnp.arange(0, half, 2, dtype=F32) / half))
        return jnp.concatenate([row[:, None] * iv[None, :], colp[:, None] * iv[None, :]], axis=-1)

    ax128 = half_tables(axial(HEAD_DIM))
    a64 = axial(MLA_ROPE)
    c, s = jnp.cos(a64), jnp.sin(a64)
    z = jnp.zeros_like(c)
    ax64 = (jnp.concatenate([c, z, c, z], axis=-1), jnp.concatenate([-s, z, s, z], axis=-1))
    return ret, ax128, ax64


def _pad_rope_cols(w):
    z = jnp.zeros(w.shape[:-1] + (MLA_ROPE // 2,), w.dtype)
    return jnp.concatenate([w[..., :MLA_ROPE // 2], z, w[..., MLA_ROPE // 2:], z], axis=-1)


def _matmul_weights(w_in, mla_w_uq, mla_w_ukv, w_out):
    depth = w_in.shape[0]
    w_in_t = jnp.swapaxes(w_in, 1, 2).astype(BF16)
    uq = mla_w_uq.reshape(depth, MLA_Q_RANK, MLA_HEADS, MLA_QK)
    uq_p = jnp.concatenate([uq[..., :MLA_NOPE], _pad_rope_cols(uq[..., MLA_NOPE:])], axis=-1)
    uq_p = uq_p.reshape(depth, MLA_Q_RANK, MLA_HEADS * MLA_PAD)
    return dict(w_in=w_in_t, wuq=uq_p.astype(BF16), wukv=mla_w_ukv.astype(BF16),
                w_out=w_out.astype(BF16))


def _layer_params(l, ln1_w, ret_decay_fwd, ret_decay_bwd, ret_gn_w, mla_q_a_norm, mla_kv_a_norm, mla_q_norm,
                  mla_k_norm, gqa_q_norm, gqa_k_norm, ln2_w):
    qn = mla_q_norm[l]
    gq = (jnp.concatenate([qn[:MLA_NOPE], _pad_rope_cols(qn[MLA_NOPE:])]) * (MLA_QK ** -0.5 * LOG2E)).reshape(1, MLA_PAD)
    kn = mla_k_norm[l]
    lg = jnp.stack([jax.nn.log_sigmoid(ret_decay_fwd[l].astype(F32)), jax.nn.log_sigmoid(ret_decay_bwd[l].astype(F32))],
                   axis=1)
    return dict(
        l=l,
        ln1=ln1_w[l].reshape(1, D_MODEL),
        dec=jnp.broadcast_to(lg[:, :, None], (RET_HEADS, 2, RET_CHUNK)),
        gn=ret_gn_w[l].reshape(1, RET_WIDTH),
        gqa_n=mla_q_a_norm[l].reshape(1, MLA_Q_RANK),
        gkva_n=mla_kv_a_norm[l].reshape(1, MLA_KV_RANK),
        gq=gq,
        gkn=kn[:MLA_NOPE].reshape(1, LANES),
        gkr=_pad_rope_cols(kn[MLA_NOPE:]).reshape(1, LANES),
        nq=(gqa_q_norm[l] * (HEAD_DIM ** -0.5 * LOG2E)).reshape(1, LANES),
        nk=gqa_k_norm[l].reshape(1, LANES),
        ln2=ln2_w[l].reshape(1, D_MODEL),
    )


def _mixer(x, w, p, tables, batch, seq, w_mlp):
    (ret_c, ret_s), (ax_c, ax_s), (m_c, m_s) = tables
    rqkvg, qm, km, vm, qg, kg, vg = _inproj(x, w, p, m_c, m_s, ax_c, ax_s, seq)
    ret = _retention(rqkvg, ret_c, ret_s, p["dec"], p["gn"], batch, seq)
    mla = _attention(qm, km, vm, batch, seq, MLA_HEADS, 1, MLA_PAD)
    gqa = _attention(qg, kg, vg, batch, seq, GQA_HEADS, GQA_GROUP, HEAD_DIM)
    return _outproj(x, ret, mla, gqa, w["w_out"], p["l"], w_mlp)


def kernel(x_prompt, x_sample, ln1_w, w_in, ret_decay_fwd, ret_decay_bwd, ret_gn_w, mla_q_a_norm, mla_w_uq,
           mla_kv_a_norm, mla_w_ukv, mla_q_norm, mla_k_norm, gqa_q_norm, gqa_k_norm, w_out, ln2_w, w_up, w_down):
    depth = w_in.shape[0]
    w = _matmul_weights(w_in, mla_w_uq, mla_w_ukv, w_out)
    groups = [x_prompt, x_sample]
    mlp_weights = [w_up, w_down]
    tables = [_rope_tables(max(x.shape[1] for x in groups))] * len(groups)
    ys = [x.reshape(x.shape[0] * x.shape[1], D_MODEL) for x in groups]
    for l in range(depth):
        p = _layer_params(l, ln1_w, ret_decay_fwd, ret_decay_bwd, ret_gn_w, mla_q_a_norm, mla_kv_a_norm, mla_q_norm,
                          mla_k_norm, gqa_q_norm, gqa_k_norm, ln2_w)
        mixed = [_mixer(y, w, p, tables[g], groups[g].shape[0], groups[g].shape[1], mlp_weights[g])
                 for g, y in enumerate(ys)]
        (_, wu), (_, wd) = mixed
        ys = [_mlp(x1, p["ln2"], wu, wd) for x1, _ in mixed]
    return tuple(y.reshape(x.shape) for y, x in zip(ys, groups))
```

```python
import functools

import jax
import jax.numpy as jnp
from jax import lax
from jax.experimental import pallas as pl
from jax.experimental.pallas import tpu as pltpu

D_MODEL = 2048
HEAD_DIM = 128
RET_HEADS = 4
RET_WIDTH = RET_HEADS * HEAD_DIM
MLA_HEADS = 4
MLA_Q_RANK = 384
MLA_KV_RANK = 256
MLA_NOPE = 128
MLA_ROPE = 64
MLA_QK = MLA_NOPE + MLA_ROPE
MLA_V = 128
GQA_HEADS = 8
GQA_KV_HEADS = 2
GQA_GROUP = GQA_HEADS // GQA_KV_HEADS
D_FF = 4 * D_MODEL
GRID_W = 64
ROPE_THETA = 10000.0
EPS = 1e-6
LOG2E = 1.4426950408889634

LANES = 128
MLA_PAD = 2 * LANES

OFF_RQ = 0
OFF_RK = 512
OFF_RV = 1024
OFF_RG = 1536
OFF_CQ = 2048
OFF_CKV = 2432
OFF_KROPE = 2688
OFF_GQ = 2752
OFF_GK = 3776
OFF_GV = 4032
IN_WIDTH = 4288
V_AUG = 2 * LANES

TM_PROJ = 512
TM_MLP = 512
TF_MLP = 1024
ROWS_ATTN = 4096
RB_ATTN = 512
TK_LONG_ATTN = 1024
TK_SHORT_ATTN = 256
RET_CHUNK = 256
VMEM_LIMIT = 52 * 1024 * 1024

F32 = jnp.float32
BF16 = jnp.bfloat16


def _params(*sem):
    return pltpu.CompilerParams(dimension_semantics=sem, vmem_limit_bytes=VMEM_LIMIT)


def _rms(x, g):
    ms = jnp.mean(x * x, axis=-1, keepdims=True)
    return x * lax.rsqrt(ms + EPS) * g


def _rope(x, c, s):
    return x * c + pltpu.roll(x, LANES // 2, 1) * s


def _inproj_kernel(x_ref, g_ref, w_ref, gqa_ref, gkva_ref, wuq_ref, wukv_ref, gq_ref, gkn_ref, gkr_ref, nq_ref, nk_ref,
                   mcos_ref, msin_ref, acos_ref, asin_ref, ret_out, qm_out, km_out, vm_out, qg_out, kg_out, vg_out):
    h = _rms(x_ref[...], g_ref[...]).astype(BF16)
    ones = jnp.ones((x_ref.shape[0], LANES), BF16)

    def proj_t(wt):
        return lax.dot_general(h, wt, (((1,), (1,)), ((), ())), preferred_element_type=F32)

    def proj(c0, c1):
        return proj_t(w_ref[c0:c1, :])

    half = MLA_ROPE // 2
    zrows = jnp.zeros((half, w_ref.shape[1]), BF16)
    pa = proj_t(jnp.concatenate([w_ref[OFF_CQ:OFF_KROPE + half, :], zrows,
                                 w_ref[OFF_KROPE + half:OFF_GQ, :], zrows], axis=0))
    kr = pa[:, OFF_KROPE - OFF_CQ:]
    pb = proj(OFF_GQ, OFF_GV)

    def cols(off, width):
        if off < OFF_KROPE:
            return pa[:, off - OFF_CQ:off - OFF_CQ + width]
        return pb[:, off - OFF_GQ:off - OFF_GQ + width]

    cqn = _rms(cols(OFF_CQ, MLA_Q_RANK), gqa_ref[...]).astype(BF16)
    qm = jnp.dot(cqn, wuq_ref[...], preferred_element_type=F32)
    ckvn = _rms(cols(OFF_CKV, MLA_KV_RANK), gkva_ref[...]).astype(BF16)
    kvm = jnp.dot(ckvn, wukv_ref[...], preferred_element_type=F32)
    cs = mcos_ref[...]
    sn = msin_ref[...]
    kr_ss =jnp.sum(kr * kr, axis=-1, keepdims=True)
    kr_rot = _rope(kr * gkr_ref[...], cs, sn)
    gq_n = gq_ref[:, :LANES]
    gq_r = gq_ref[:, LANES:]
    gk_n = gkn_ref[...]
    inv_d = 1.0 / MLA_QK
    for hd in range(MLA_HEADS):
        o = MLA_PAD * hd
        qn = qm[:, o:o + LANES]
        qr = qm[:, o + LANES:o + MLA_PAD]
        rq = lax.rsqrt((jnp.sum(qn * qn, axis=-1, keepdims=True) + jnp.sum(qr * qr, axis=-1, keepdims=True)) * inv_d
                       + EPS)
        qm_out[:, o:o + LANES] = (qn * rq * gq_n).astype(BF16)
        qm_out[:, o + LANES:o + MLA_PAD] = _rope(qr * rq * gq_r, cs, sn).astype(BF16)
        kn = kvm[:, o:o + LANES]
        rk = lax.rsqrt((jnp.sum(kn * kn, axis=-1, keepdims=True) + kr_ss) * inv_d + EPS)
        km_out[:, o:o + LANES] = (kn * rk * gk_n).astype(BF16)
        km_out[:, o + LANES:o + MLA_PAD] = (kr_rot * rk).astype(BF16)
        vm_out[:, o:o + LANES] = kvm[:, o + LANES:o + MLA_PAD].astype(BF16)
        vm_out[:, o + LANES:o + V_AUG] = ones

    cs = acos_ref[...]
    sn = asin_ref[...]
    nq = nq_ref[...]
    nk = nk_ref[...]
    for hd in range(GQA_HEADS):
        qg_out[:, LANES * hd:LANES * (hd + 1)] = _rope(_rms(cols(OFF_GQ + LANES * hd, LANES), nq), cs, sn).astype(BF16)
    for hd in range(GQA_KV_HEADS):
        kg_out[:, LANES * hd:LANES * (hd + 1)] = _rope(_rms(cols(OFF_GK + LANES * hd, LANES), nk), cs, sn).astype(BF16)

    gv = proj(OFF_GV, IN_WIDTH).astype(BF16)
    for hd in range(GQA_KV_HEADS):
        vg_out[:, V_AUG * hd:V_AUG * hd + LANES] = gv[:, LANES * hd:LANES * (hd + 1)]
        vg_out[:, V_AUG * hd + LANES:V_AUG * (hd + 1)] = ones

    for c0 in range(0, OFF_CQ, 1024):
        ret_out[:, c0:c0 + 1024] = proj(c0, c0 + 1024).astype(BF16)


def _inproj(x, w, p, mcos, msin, acos, asin, seq):
    m = x.shape[0]
    tm = TM_PROJ
    nseq = seq // tm
    l = p["l"]
    full = lambda shape: pl.BlockSpec(shape, lambda i: (0, 0))
    layer = lambda shape: pl.BlockSpec((None,) + shape, lambda i: (l, 0, 0))
    table = pl.BlockSpec((tm, LANES), lambda i: (i % nseq, 0))
    widths = (OFF_CQ, MLA_HEADS * MLA_PAD, MLA_HEADS * MLA_PAD, MLA_HEADS * V_AUG, GQA_HEADS * HEAD_DIM,
              GQA_KV_HEADS * HEAD_DIM, GQA_KV_HEADS * V_AUG)
    return pl.pallas_call(
        _inproj_kernel,
        grid=(m // tm,),
        in_specs=[
            pl.BlockSpec((tm, D_MODEL), lambda i: (i, 0)),
            full((1, D_MODEL)),
            pl.BlockSpec((None, IN_WIDTH, D_MODEL), lambda i: (l, 0, 0), pipeline_mode=pl.Buffered(1)),
            full((1, MLA_Q_RANK)),
            full((1, MLA_KV_RANK)),
            layer((MLA_Q_RANK, MLA_HEADS * MLA_PAD)),
            layer((MLA_KV_RANK, MLA_HEADS * MLA_PAD)),
            full((1, MLA_PAD)),
            full((1, LANES)),
            full((1, LANES)),
            full((1, LANES)),
            full((1, LANES)),
            table, table, table, table,
        ],
        out_specs=[pl.BlockSpec((tm, wd), lambda i: (i, 0)) for wd in widths],
        out_shape=[jax.ShapeDtypeStruct((m, wd), BF16) for wd in widths],
        compiler_params=_params("parallel"),
        name="in_proj",
    )(x, p["ln1"], w["w_in"], p["gqa_n"], p["gkva_n"], w["wuq"], w["wukv"], p["gq"], p["gkn"], p["gkr"], p["nq"],
      p["nk"], mcos, msin, acos, asin)


def _ret_kernel(q_ref, k_ref, v_ref, g_ref, cos_ref, sin_ref, dec_ref, gn_ref, o_ref, kv_scr, st_scr, k_scr, *, chunk):
    seq = q_ref.shape[0]
    n_chunks = seq // chunk
    lgf = dec_ref[0, 0:1, :]
    lgb = dec_ref[0, 1:2, :]
    lgf_l = lgf[:, :LANES]
    lgb_l = lgb[:, :LANES]
    k_scale = HEAD_DIM ** -0.5
    pos = lax.broadcasted_iota(jnp.int32, (chunk, LANES), 0).astype(F32)
    zeta_f = jnp.exp((chunk - 1.0 - pos) * lgf_l)
    zeta_b = jnp.exp(pos * lgb_l)
    inner_f = jnp.exp((pos + 1.0) * lgf_l) * k_scale
    inner_b = jnp.exp((chunk - pos) * lgb_l) * k_scale
    ii = lax.broadcasted_iota(jnp.int32, (chunk, chunk), 0)
    jj = lax.broadcasted_iota(jnp.int32, (chunk, chunk), 1)
    d = (ii - jj).astype(F32)
    dmask = jnp.where(d >= 0, jnp.exp(jnp.maximum(d, 0.0) * lgf), jnp.exp(jnp.maximum(-d, 0.0) * lgb)) * k_scale

    def rows(c):
        return slice(c * chunk, (c + 1) * chunk)

    def chunk_kv(c):
        r = rows(c)
        k = _rope(k_ref[r, :].astype(F32), cos_ref[r, :], sin_ref[r, :])
        k_scr[r, :] = k.astype(BF16)
        v = v_ref[r, :].astype(F32)
        vz = jnp.concatenate([v * zeta_f, v * zeta_b], axis=1).astype(BF16)
        kt = k.T.astype(BF16)
        kv_scr[c] = jnp.dot(kt, vz, preferred_element_type=F32)

    for c in range(n_chunks):
        chunk_kv(c)

    dec_f = jnp.exp(chunk * lgf_l)
    dec_b = jnp.exp(chunk * lgb_l)
    sf = jnp.zeros((HEAD_DIM, LANES), F32)
    sb = jnp.zeros((HEAD_DIM, LANES), F32)
    for c in range(n_chunks):
        cb = n_chunks - 1 - c
        st_scr[c, :, :LANES] = sf.astype(BF16)
        st_scr[cb, :, LANES:] = sb.astype(BF16)
        sf = dec_f * sf + kv_scr[c, :, :LANES]
        sb = dec_b * sb + kv_scr[cb, :, LANES:]

    gn = gn_ref[...]

    def chunk_out(c):
        r = rows(c)
        q = _rope(q_ref[r, :].astype(F32), cos_ref[r, :], sin_ref[r, :]).astype(BF16)
        a = lax.dot_general(q, k_scr[r, :], (((1,), (1,)), ((), ())), preferred_element_type=F32) * dmask
        y = jnp.dot(a.astype(BF16), v_ref[r, :].astype(BF16), preferred_element_type=F32)
        cr = jnp.dot(q, st_scr[c], preferred_element_type=F32)
        y = y + cr[:, :LANES] * inner_f + cr[:, LANES:] * inner_b
        mu = jnp.mean(y, axis=-1, keepdims=True)
        yc = y - mu
        var = jnp.mean(yc * yc, axis=-1, keepdims=True)
        yn = yc * lax.rsqrt(var + EPS)
        g = g_ref[r, :].astype(F32)
        silu = g * (1.0 / (1.0 + jnp.exp(-g)))
        o_ref[r, :] = (yn * gn * silu).astype(o_ref.dtype)

    for c in range(n_chunks):
        chunk_out(c)


def _retention(proj, cos, sin, dec, gn, batch, seq):
    m = proj.shape[0]
    n_chunks = seq // RET_CHUNK
    col = lambda off: (lambda b, h: (b, off // LANES + h))
    return pl.pallas_call(
        functools.partial(_ret_kernel, chunk=RET_CHUNK),
        grid=(batch, RET_HEADS),
        in_specs=[
            pl.BlockSpec((seq, LANES), col(OFF_RQ)),
            pl.BlockSpec((seq, LANES), col(OFF_RK)),
            pl.BlockSpec((seq, LANES), col(OFF_RV)),
            pl.BlockSpec((seq, LANES), col(OFF_RG)),
            pl.BlockSpec((seq, LANES), lambda b, h: (0, 0)),
            pl.BlockSpec((seq, LANES), lambda b, h: (0, 0)),
            pl.BlockSpec((1, 2, RET_CHUNK), lambda b, h: (h, 0, 0)),
            pl.BlockSpec((1, LANES), lambda b, h: (0, h)),
        ],
        out_specs=pl.BlockSpec((seq, LANES), lambda b, h: (b, h)),
        out_shape=jax.ShapeDtypeStruct((m, RET_WIDTH), BF16),
        scratch_shapes=[
            pltpu.VMEM((n_chunks, HEAD_DIM, 2 * LANES), F32),
            pltpu.VMEM((n_chunks, HEAD_DIM, 2 * LANES), BF16),
            pltpu.VMEM((seq, HEAD_DIM), BF16),
        ],
        compiler_params=_params("parallel", "parallel"),
        name="retention",
    )(proj, proj, proj, proj, cos, sin, dec, gn)


def _attn_kernel(q_ref, k_ref, v_ref, o_ref, *, tk, rb, d_qk, kv_heads):
    seq = k_ref.shape[0]
    nr = q_ref.shape[0] // rb
    q_heads = q_ref.shape[1] // d_qk
    for hd in range(q_heads):
        kv = hd * kv_heads // q_heads
        kc = slice(kv * d_qk, (kv + 1) * d_qk)
        vc = slice(kv * V_AUG, (kv + 1) * V_AUG)
        for r in range(nr):
            rows = slice(r * rb, (r + 1) * rb)
            q = q_ref[rows, hd * d_qk:(hd + 1) * d_qk]
            m = acc = None
            for j in range(seq // tk):
                ks = slice(j * tk, (j + 1) * tk)
                s = lax.dot_general(q, k_ref[ks, kc], (((1,), (1,)), ((), ())), preferred_element_type=F32)
                mj = jnp.max(s, axis=-1, keepdims=True)
                if j == 0:
                    m = mj
                    acc = jnp.dot(jnp.exp2(s - m).astype(BF16), v_ref[ks, vc], preferred_element_type=F32)
                else:
                    m_new = jnp.maximum(m, mj)
                    alpha = jnp.exp2(m - m_new)
                    acc = alpha * acc + jnp.dot(jnp.exp2(s - m_new).astype(BF16), v_ref[ks, vc],
                                                preferred_element_type=F32)
                    m = m_new
            o_ref[rows, hd * LANES:(hd + 1) * LANES] = (acc[:, :LANES] / acc[:, LANES:]).astype(o_ref.dtype)


def _attention(q, k, v, batch, seq, heads, group, d_qk):
    m = q.shape[0]
    tq = min(seq, ROWS_ATTN * LANES // d_qk)
    nq = seq // tq
    hps = ROWS_ATTN // tq if nq == 1 else 1
    kv_heads = 1 if group >= hps else hps // group
    d_v = HEAD_DIM
    return pl.pallas_call(
        functools.partial(_attn_kernel, tk=TK_LONG_ATTN if seq >= ROWS_ATTN else TK_SHORT_ATTN, rb=RB_ATTN,
                          d_qk=d_qk, kv_heads=kv_heads),
        grid=(batch, heads // hps, nq),
        in_specs=[
            pl.BlockSpec((tq, hps * d_qk), lambda b, h, i: (b * nq + i, h)),
            pl.BlockSpec((seq, kv_heads * d_qk), lambda b, h, i: (b, h * hps // (group * kv_heads))),
            pl.BlockSpec((seq, kv_heads * V_AUG), lambda b, h, i: (b, h * hps // (group * kv_heads))),
        ],
        out_specs=pl.BlockSpec((tq, hps * d_v), lambda b, h, i: (b * nq + i, h)),
        out_shape=jax.ShapeDtypeStruct((m, heads * d_v), BF16),
        compiler_params=_params("parallel", "parallel", "parallel"),
        name="attention",
    )(q, k, v)


def _outproj_kernel(x_ref, a_ref, b_ref, c_ref, w_ref, wsrc_ref, o_ref, wdst_ref):
    wa = a_ref.shape[1]
    wb = b_ref.shape[1]
    acc = x_ref[...]
    acc = acc + jnp.dot(a_ref[...], w_ref[0:wa, :], preferred_element_type=F32)
    acc = acc + jnp.dot(b_ref[...], w_ref[wa:wa + wb, :], preferred_element_type=F32)
    acc = acc + jnp.dot(c_ref[...], w_ref[wa + wb:, :], preferred_element_type=F32)
    o_ref[...] = acc
    wdst_ref[...] = wsrc_ref[...].astype(BF16)


def _outproj(x, a, b, c, w, l, w_mlp):
    m = x.shape[0]
    tm = TM_PROJ
    steps = m // tm
    _, w_rows, w_cols = w_mlp.shape
    assert w_rows % steps == 0
    slab = w_rows // steps
    row = lambda width: pl.BlockSpec((tm, width), lambda i: (i, 0))
    return pl.pallas_call(
        _outproj_kernel,
        grid=(steps,),
        in_specs=[
            row(D_MODEL), row(a.shape[1]), row(b.shape[1]), row(c.shape[1]),
            pl.BlockSpec((None, D_MODEL, D_MODEL), lambda i: (l, 0, 0), pipeline_mode=pl.Buffered(1)),
            pl.BlockSpec((None, slab, w_cols), lambda i: (l, i, 0)),
        ],
        out_specs=[row(D_MODEL), pl.BlockSpec((slab, w_cols), lambda i: (i, 0))],
        out_shape=[jax.ShapeDtypeStruct((m, D_MODEL), F32), jax.ShapeDtypeStruct((w_rows, w_cols), BF16)],
        compiler_params=_params("parallel"),
        name="out_proj",
    )(x, a, b, c, w, w_mlp)


def _mlp_kernel(x_ref, g_ref, wu_ref, wd_ref, o_ref, h_scr):
    @pl.when(pl.program_id(1) == 0)
    def _():
        x = x_ref[...]
        h_scr[...] = _rms(x, g_ref[...]).astype(BF16)
        o_ref[...] = x

    u = jnp.dot(h_scr[...], wu_ref[...], preferred_element_type=F32)
    u = jnp.square(jnp.maximum(u, 0.0)).astype(BF16)
    o_ref[...] += jnp.dot(u, wd_ref[...], preferred_element_type=F32)


def _mlp(x, g, wu, wd):
    m = x.shape[0]
    tm = TM_MLP
    tf = TF_MLP
    return pl.pallas_call(
        _mlp_kernel,
        grid=(m // tm, D_FF // tf),
        in_specs=[
            pl.BlockSpec((tm, D_MODEL), lambda i, f: (i, 0)),
            pl.BlockSpec((1, D_MODEL), lambda i, f: (0, 0)),
            pl.BlockSpec((D_MODEL, tf), lambda i, f: (0, f)),
            pl.BlockSpec((tf, D_MODEL), lambda i, f: (f, 0)),
        ],
        out_specs=pl.BlockSpec((tm, D_MODEL), lambda i, f: (i, 0)),
        out_shape=jax.ShapeDtypeStruct((m, D_MODEL), F32),
        scratch_shapes=[pltpu.VMEM((tm, D_MODEL), BF16)],
        compiler_params=_params("parallel", "arbitrary"),
        name="mlp",
    )(x, g, wu, wd)


def _rope_tables(seq):
    t = jnp.arange(seq, dtype=F32)

    def half_tables(ang):
        c, s = jnp.cos(ang), jnp.sin(ang)
        return jnp.concatenate([c, c], axis=-1), jnp.concatenate([-s, s], axis=-1)

    inv = 1.0 / (ROPE_THETA ** (jnp.arange(0, HEAD_DIM, 2, dtype=F32) / HEAD_DIM))
    ret = half_tables(t[:, None] * inv[None, :])

    ti = jnp.arange(seq)
    row = (ti // GRID_W).astype(F32)
    colp = (ti % GRID_W).astype(F32)

    def axial(dim):
        half = dim // 2
        iv = 1.0 / (ROPE_THETA ** (jnp.arange(0, half, 2, dtype=F32) / half))
        return jnp.concatenate([row[:, None] * iv[None, :], colp[:, None] * iv[None, :]], axis=-1)

    ax128 = half_tables(axial(HEAD_DIM))
    a64 = axial(MLA_ROPE)
    c, s = jnp.cos(a64), jnp.sin(a64)
    z = jnp.zeros_like(c)
    ax64 = (jnp.concatenate([c, z, c, z], axis=-1), jnp.concatenate([-s, z, s, z], axis=-1))
    return ret, ax128, ax64


def _pad_rope_cols(w):
    z = jnp.zeros(w.shape[:-1] + (MLA_ROPE // 2,), w.dtype)
    return jnp.concatenate([w[..., :MLA_ROPE // 2], z, w[..., MLA_ROPE // 2:], z], axis=-1)


def _matmul_weights(w_in, mla_w_uq, mla_w_ukv, w_out):
    depth = w_in.shape[0]
    w_in_t = jnp.swapaxes(w_in, 1, 2).astype(BF16)
    uq = mla_w_uq.reshape(depth, MLA_Q_RANK, MLA_HEADS, MLA_QK)
    uq_p = jnp.concatenate([uq[..., :MLA_NOPE], _pad_rope_cols(uq[..., MLA_NOPE:])], axis=-1)
    uq_p = uq_p.reshape(depth, MLA_Q_RANK, MLA_HEADS * MLA_PAD)
    return dict(w_in=w_in_t, wuq=uq_p.astype(BF16), wukv=mla_w_ukv.astype(BF16),
                w_out=w_out.astype(BF16))


def _layer_params(l, ln1_w, ret_decay_fwd, ret_decay_bwd, ret_gn_w, mla_q_a_norm, mla_kv_a_norm, mla_q_norm,
                  mla_k_norm, gqa_q_norm, gqa_k_norm, ln2_w):
    qn = mla_q_norm[l]
    gq = (jnp.concatenate([qn[:MLA_NOPE], _pad_rope_cols(qn[MLA_NOPE:])]) * (MLA_QK ** -0.5 * LOG2E)).reshape(1, MLA_PAD)
    kn = mla_k_norm[l]
    lg = jnp.stack([jax.nn.log_sigmoid(ret_decay_fwd[l].astype(F32)), jax.nn.log_sigmoid(ret_decay_bwd[l].astype(F32))],
                   axis=1)
    return dict(
        l=l,
        ln1=ln1_w[l].reshape(1, D_MODEL),
        dec=jnp.broadcast_to(lg[:, :, None], (RET_HEADS, 2, RET_CHUNK)),
        gn=ret_gn_w[l].reshape(1, RET_WIDTH),
        gqa_n=mla_q_a_norm[l].reshape(1, MLA_Q_RANK),
        gkva_n=mla_kv_a_norm[l].reshape(1, MLA_KV_RANK),
        gq=gq,
        gkn=kn[:MLA_NOPE].reshape(1, LANES),
        gkr=_pad_rope_cols(kn[MLA_NOPE:]).reshape(1, LANES),
        nq=(gqa_q_norm[l] * (HEAD_DIM ** -0.5 * LOG2E)).reshape(1, LANES),
        nk=gqa_k_norm[l].reshape(1, LANES),
        ln2=ln2_w[l].reshape(1, D_MODEL),
    )


def _mixer(x, w, p, tables, batch, seq, w_mlp):
    (ret_c, ret_s), (ax_c, ax_s), (m_c, m_s) = tables
    rqkvg, qm, km, vm, qg, kg, vg = _inproj(x, w, p, m_c, m_s, ax_c, ax_s, seq)
    ret = _retention(rqkvg, ret_c, ret_s, p["dec"], p["gn"], batch, seq)
    mla = _attention(qm, km, vm, batch, seq, MLA_HEADS, 1, MLA_PAD)
    gqa = _attention(qg, kg, vg, batch, seq, GQA_HEADS, GQA_GROUP, HEAD_DIM)
    return _outproj(x, ret, mla, gqa, w["w_out"], p["l"], w_mlp)


def kernel(x_prompt, x_sample, ln1_w, w_in, ret_decay_fwd, ret_decay_bwd, ret_gn_w, mla_q_a_norm, mla_w_uq,
           mla_kv_a_norm, mla_w_ukv, mla_q_norm, mla_k_norm, gqa_q_norm, gqa_k_norm, w_out, ln2_w, w_up, w_down):
    depth = w_in.shape[0]
    w = _matmul_weights(w_in, mla_w_uq, mla_w_ukv, w_out)
    groups = [x_prompt, x_sample]
    mlp_weights = [w_up, w_down]
    tables = [_rope_tables(max(x.shape[1] for x in groups))] * len(groups)
    ys = [x.reshape(x.shape[0] * x.shape[1], D_MODEL) for x in groups]
    for l in range(depth):
        p = _layer_params(l, ln1_w, ret_decay_fwd, ret_decay_bwd, ret_gn_w, mla_q_a_norm, mla_kv_a_norm, mla_q_norm,
                          mla_k_norm, gqa_q_norm, gqa_k_norm, ln2_w)
        mixed = [_mixer(y, w, p, tables[g], groups[g].shape[0], groups[g].shape[1], mlp_weights[g])
                 for g, y in enumerate(ys)]
        (_, wu), (_, wd) = mixed
        ys = [_mlp(x1, p["ln2"], wu, wd) for x1, _ in mixed]
    return tuple(y.reshape(x.shape) for y, x in zip(ys, groups))
```
